```python
import jax, jax.numpy as jnp
from jax import lax
import numpy as np

D_MODEL = 1024
BATCH = 8
SEQ = 2048
DEPTH = 4

GRID_W = 64
CTX_LEN = 256
CHUNK = 128
A_WIDTH = 1024
A_GROUPS = 4
N_HEADS = 16
N_KV_HEADS = 4
HEAD_DIM = 64
GQA_GROUP = N_HEADS // N_KV_HEADS
Q_DIM = N_HEADS * HEAD_DIM
KV_DIM = N_KV_HEADS * HEAD_DIM
Q_BLOCK = 128
ROPE_THETA = 10000.0
C_WIDTH = 1024
CONV_W = 31
N_BRANCH = 3
IN_COLS = 2 * A_WIDTH + Q_DIM + 2 * KV_DIM + 2 * C_WIDTH + N_BRANCH * D_MODEL
PEER_HEADS = 8
N_KEYS = 128
N_EXPERTS = N_KEYS * N_KEYS
PEER_TOPK = 16
PEER_QDIM = 256
PEER_HALF = PEER_QDIM // 2
PEER_BLOCK = 128
ALPHA = float((2 * DEPTH) ** 0.25)
BETA = float((8 * DEPTH) ** -0.25)
EPS = 1e-6

kernel_name = 'hybrid_gmlp_gqa_conformer_peer_dit'


def _layer_norm(x, g=None, b=None):
    xf = x.astype(jnp.float32)
    mu = jnp.mean(xf, -1, keepdims=True)
    var = jnp.mean(jnp.square(xf - mu), -1, keepdims=True)
    y = (xf - mu) * lax.rsqrt(var + EPS)
    if g is not None:
        y = y * g.astype(jnp.float32) + b.astype(jnp.float32)
    return y.astype(x.dtype)


def _rms_norm(x, g):
    xf = x.astype(jnp.float32)
    y = xf * lax.rsqrt(jnp.mean(xf * xf, -1, keepdims=True) + EPS)
    return (y * g.astype(jnp.float32)).astype(x.dtype)


def _axial_rope(n_tokens):
    rows = n_tokens // GRID_W
    row = jnp.repeat(jnp.arange(rows, dtype=jnp.float32), GRID_W)
    col = jnp.tile(jnp.arange(GRID_W, dtype=jnp.float32), rows)
    n_freq = HEAD_DIM // 4
    inv = ROPE_THETA ** (-jnp.arange(n_freq, dtype=jnp.float32) / n_freq)
    ang = jnp.concatenate([row[:, None] * inv, col[:, None] * inv], -1)
    return jnp.cos(ang), jnp.sin(ang)


def _apply_rope(x, cos, sin):
    xf = x.astype(jnp.float32)
    x1, x2 = xf[..., 0::2], xf[..., 1::2]
    cs, sn = cos[None, :, None, :], sin[None, :, None, :]
    y = jnp.stack([x1 * cs - x2 * sn, x1 * sn + x2 * cs], -1).reshape(x.shape)
    return y.astype(x.dtype)


def _heads(z, n):
    return z.reshape(*z.shape[:-1], n, HEAD_DIM)


def _split_proj(h, w_in):
    z = h @ w_in
    sizes = [A_WIDTH, A_WIDTH, Q_DIM, KV_DIM, KV_DIM, C_WIDTH, C_WIDTH]
    return jnp.split(z, [int(s) for s in np.cumsum(sizes)], axis=-1)


def _attend(q, k, v):
    s = jnp.einsum('bqhgd,bkhd->bhgqk', q, k).astype(jnp.float32) * (HEAD_DIM ** -0.5)
    p = jax.nn.softmax(s, axis=-1).astype(v.dtype)
    return jnp.einsum('bhgqk,bkhd->bqhgd', p, v)


def _latent_attention(q, k_all, v_all):
    bsz, n = q.shape[0], q.shape[1]
    nb = n // Q_BLOCK
    qb = q.reshape(bsz, nb, Q_BLOCK, N_KV_HEADS, GQA_GROUP, HEAD_DIM).transpose(1, 0, 2, 3, 4, 5)
    ob = lax.map(lambda qq: _attend(qq, k_all, v_all), qb)
    return ob.transpose(1, 0, 2, 3, 4, 5).reshape(bsz, n, Q_DIM)


def _context_attention(q, k, v):
    bsz, n = q.shape[0], q.shape[1]
    qg = q.reshape(bsz, n, N_KV_HEADS, GQA_GROUP, HEAD_DIM)
    return _attend(qg, k, v).reshape(bsz, n, Q_DIM)


def _chunk_gmlp(u, v, ln_g, ln_b, ws, bs):
    bsz, n, _ = u.shape
    u = jax.nn.gelu(u)
    v = _layer_norm(jax.nn.gelu(v), ln_g, ln_b)
    vc = v.reshape(bsz, n // CHUNK, CHUNK, A_GROUPS, A_WIDTH // A_GROUPS)
    s = jnp.einsum('gij,bnjgc->bnigc', ws, vc) + bs.T[None, None, :, :, None]
    return u * s.reshape(bsz, n, A_WIDTH)


def _conformer_conv(a, b, w, bias, ln_g, ln_b):
    z = a * jax.nn.sigmoid(b)
    y = lax.conv_general_dilated(z, w[:, None, :], window_strides=(1,),
                                 padding=[(CONV_W // 2, CONV_W // 2)],
                                 dimension_numbers=('NWC', 'WIO', 'NWC'),
                                 feature_group_count=C_WIDTH) + bias
    return jax.nn.silu(_layer_norm(y, ln_g, ln_b))


def _merge(ha, hb, hc, gates, a_wp, b_wp, c_wp, w_out):
    g = jax.nn.sigmoid(gates.astype(jnp.float32)).astype(ha.dtype)
    g = g.reshape(*gates.shape[:-1], N_BRANCH, D_MODEL)
    m = g[..., 0, :] * (ha @ a_wp) + g[..., 1, :] * (hb @ b_wp) + g[..., 2, :] * (hc @ c_wp)
    return m @ w_out


def _peer(t, wq, keys, u_tab, v_tab):
    n = t.shape[0]
    tb = t.reshape(n // PEER_BLOCK, PEER_BLOCK, D_MODEL)

    def block(xb):
        q = (xb @ wq).reshape(PEER_BLOCK, PEER_HEADS, 2, PEER_HALF)
        s = jnp.einsum('thpd,pkd->thpk', q, keys).astype(jnp.float32)
        sv, si = lax.top_k(s, PEER_TOPK)
        cand = (sv[:, :, 0, :, None] + sv[:, :, 1, None, :]).reshape(PEER_BLOCK, PEER_HEADS, PEER_TOPK * PEER_TOPK)
        cidx = (si[:, :, 0, :, None] * N_KEYS + si[:, :, 1, None, :]).reshape(PEER_BLOCK, PEER_HEADS, PEER_TOPK * PEER_TOPK)
        top_s, pos = lax.top_k(cand, PEER_TOPK)
        eid = jnp.take_along_axis(cidx, pos, axis=-1)
        gate = jax.nn.softmax(top_s, axis=-1)
        ue = u_tab[eid]
        ve = v_tab[eid]
        act = jax.nn.gelu(jnp.einsum('thkd,td->thk', ue, xb).astype(jnp.float32)) * gate
        return jnp.einsum('thk,thkd->td', act.astype(ve.dtype), ve)

    return lax.map(block, tb).reshape(n, D_MODEL)


def setup_inputs(seed: int = 0) -> dict:
    key = jax.random.key(seed)
    ks = iter(jax.random.split(key, 40))

    def nrm(shape, scale):
        return jax.random.normal(next(ks), shape, jnp.float32) * scale

    L, D = DEPTH, D_MODEL
    return {
        'x': nrm((BATCH, SEQ, D), 1.0),
        'c': nrm((BATCH, D), 1.0),
        'ctx': nrm((BATCH, CTX_LEN, D), 1.0),
        'c_ctx': nrm((D,), 1.0),
        'w_mod': nrm((L, D, 6 * D), D ** -0.5),
        'b_mod': nrm((L, 6 * D), 0.01),
        'w_in': nrm((L, D, IN_COLS), D ** -0.5),
        'a_ln_g': 1.0 + nrm((L, A_WIDTH), 0.02),
        'a_ln_b': nrm((L, A_WIDTH), 0.01),
        'a_ws': nrm((L, A_GROUPS, CHUNK, CHUNK), CHUNK ** -0.5),
        'a_bs': 1.0 + nrm((L, A_GROUPS, CHUNK), 0.01),
        'a_wp': nrm((L, A_WIDTH, D), BETA * A_WIDTH ** -0.5),
        'q_norm_g': 1.0 + nrm((L, HEAD_DIM), 0.02),
        'k_norm_g': 1.0 + nrm((L, HEAD_DIM), 0.02),
        'b_wp': nrm((L, Q_DIM, D), BETA * Q_DIM ** -0.5),
        'c_conv_w': nrm((L, CONV_W, C_WIDTH), CONV_W ** -0.5),
        'c_conv_b': nrm((L, C_WIDTH), 0.01),
        'c_ln_g': 1.0 + nrm((L, C_WIDTH), 0.02),
        'c_ln_b': nrm((L, C_WIDTH), 0.01),
        'c_wp': nrm((L, C_WIDTH, D), BETA * C_WIDTH ** -0.5),
        'w_out': nrm((L, D, D), BETA * D ** -0.5),
        'ln1_g': 1.0 + nrm((L, D), 0.02),
        'ln1_b': nrm((L, D), 0.01),
        'peer_wq': nrm((L, D, PEER_HEADS * PEER_QDIM), D ** -0.5),
        'peer_keys': nrm((L, 2, N_KEYS, PEER_HALF), PEER_HALF ** -0.5),
        'peer_u': nrm((L, N_EXPERTS, D), D ** -0.5),
        'peer_v': nrm((L, N_EXPERTS, D), BETA * PEER_HEADS ** -0.5),
        'ln2_g': 1.0 + nrm((L, D), 0.02),
        'ln2_b': nrm((L, D), 0.01),
    }


def reference(x, c, ctx, c_ctx, w_mod, b_mod, w_in, a_ln_g, a_ln_b, a_ws, a_bs, a_wp,
              q_norm_g, k_norm_g, b_wp, c_conv_w, c_conv_b, c_ln_g, c_ln_b, c_wp, w_out,
              ln1_g, ln1_b, peer_wq, peer_keys, peer_u, peer_v, ln2_g, ln2_b):
    bsz, n_lat, _ = x.shape
    n_ctx = ctx.shape[1]
    cos, sin = _axial_rope(n_lat)
    xc = ctx
    for i in range(DEPTH):
        last = i == DEPTH - 1
        mod = jax.nn.silu(c) @ w_mod[i] + b_mod[i]
        mod_c = jax.nn.silu(c_ctx) @ w_mod[i] + b_mod[i]
        sh1, sc1, g1, sh2, sc2, g2 = jnp.split(mod[:, None, :], 6, axis=-1)
        csh1, csc1, cg1, csh2, csc2, cg2 = jnp.split(mod_c, 6, axis=-1)

        h = _layer_norm(x) * (1.0 + sc1) + sh1
        hc = _layer_norm(xc) * (1.0 + csc1) + csh1
        ua, va, q, k, v, ga, gb, gt = _split_proj(h, w_in[i])
        uac, vac, qc, kc, vc, gac, gbc, gtc = _split_proj(hc, w_in[i])

        q = _apply_rope(_rms_norm(_heads(q, N_HEADS), q_norm_g[i]), cos, sin)
        k = _apply_rope(_rms_norm(_heads(k, N_KV_HEADS), k_norm_g[i]), cos, sin)
        v = _heads(v, N_KV_HEADS)
        kc = _rms_norm(_heads(kc, N_KV_HEADS), k_norm_g[i])
        vc = _heads(vc, N_KV_HEADS)
        att = _latent_attention(q, jnp.concatenate([kc, k], axis=1), jnp.concatenate([vc, v], axis=1))
        y = _merge(_chunk_gmlp(ua, va, a_ln_g[i], a_ln_b[i], a_ws[i], a_bs[i]),
                   att,
                   _conformer_conv(ga, gb, c_conv_w[i], c_conv_b[i], c_ln_g[i], c_ln_b[i]),
                   gt, a_wp[i], b_wp[i], c_wp[i], w_out[i])
        x = _layer_norm(ALPHA * x + g1 * y, ln1_g[i], ln1_b[i])

        if not last:
            qc = _rms_norm(_heads(qc, N_HEADS), q_norm_g[i])
            yc = _merge(_chunk_gmlp(uac, vac, a_ln_g[i], a_ln_b[i], a_ws[i], a_bs[i]),
                        _context_attention(qc, kc, vc),
                        _conformer_conv(gac, gbc, c_conv_w[i], c_conv_b[i], c_ln_g[i], c_ln_b[i]),
                        gtc, a_wp[i], b_wp[i], c_wp[i], w_out[i])
            xc = _layer_norm(ALPHA * xc + cg1 * yc, ln1_g[i], ln1_b[i])

        h2 = (_layer_norm(x) * (1.0 + sc2) + sh2).reshape(bsz * n_lat, D_MODEL)
        if last:
            y2 = _peer(h2, peer_wq[i], peer_keys[i], peer_u[i], peer_v[i]).reshape(x.shape)
        else:
            h2c = (_layer_norm(xc) * (1.0 + csc2) + csh2).reshape(bsz * n_ctx, D_MODEL)
            out = _peer(jnp.concatenate([h2, h2c], axis=0), peer_wq[i], peer_keys[i], peer_u[i], peer_v[i])
            y2 = out[:bsz * n_lat].reshape(x.shape)
            y2c = out[bsz * n_lat:].reshape(xc.shape)
            xc = _layer_norm(ALPHA * xc + cg2 * y2c, ln2_g[i], ln2_b[i])
        x = _layer_norm(ALPHA * x + g2 * y2, ln2_g[i], ln2_b[i])
    return x
```

```python
import functools

import jax
import jax.numpy as jnp
import numpy as np
from jax import lax
from jax.experimental import pallas as pl
from jax.experimental.pallas import tpu as pltpu

D_MODEL = 1024
GRID_W = 64
CHUNK = 128
A_WIDTH = 1024
A_GROUPS = 4
N_HEADS = 16
N_KV_HEADS = 4
HEAD_DIM = 64
Q_DIM = N_HEADS * HEAD_DIM
KV_DIM = N_KV_HEADS * HEAD_DIM
ROPE_THETA = 10000.0
C_WIDTH = 1024
CONV_W = 31
CONV_HALO = 16
PEER_HEADS = 8
N_KEYS = 128
N_EXPERTS = N_KEYS * N_KEYS
PEER_TOPK = 16
PEER_QDIM = 256
PEER_HALF = PEER_QDIM // 2
EPS = 1e-6
MOD_ROWS = 16
LANES = 128

BF16 = jnp.bfloat16
F32 = jnp.float32

VMEM_LIMIT_BYTES = 56 * 1024 * 1024


def _params(*sem):
    return pltpu.CompilerParams(dimension_semantics=sem, vmem_limit_bytes=VMEM_LIMIT_BYTES)


def _layer_norm(x):
    mu = jnp.mean(x, -1, keepdims=True)
    xc = x - mu
    var = jnp.mean(xc * xc, -1, keepdims=True)
    return xc * lax.rsqrt(var + EPS)


def _split_bf16(a):
    hi = a.astype(BF16)
    lo = (a - hi.astype(F32)).astype(BF16)
    return hi, lo


def _dot(a, b):
    return jnp.dot(a, b, preferred_element_type=F32)


def _dot_nt(a, b):
    return lax.dot_general(a, b, (((1,), (1,)), ((), ())), preferred_element_type=F32)


def _mod_kernel(c_ref, w_ref, b_ref, o_ref):
    c = c_ref[...]
    a_hi, a_lo = _split_bf16(c * jax.nn.sigmoid(c))
    w_hi, w_lo = _split_bf16(w_ref[0])
    o_ref[0] = _dot(a_hi, w_hi) + _dot(a_hi, w_lo) + _dot(a_lo, w_hi) + b_ref[0]


def _modulation(c_rows, w_mod, b_mod):
    depth, d, n = w_mod.shape
    tn = 1536
    return pl.pallas_call(
        _mod_kernel,
        grid=(depth, n // tn),
        in_specs=[
            pl.BlockSpec((MOD_ROWS, d), lambda l, j: (0, 0)),
            pl.BlockSpec((1, d, tn), lambda l, j: (l, 0, j)),
            pl.BlockSpec((1, 1, tn), lambda l, j: (l, 0, j)),
        ],
        out_specs=pl.BlockSpec((1, MOD_ROWS, tn), lambda l, j: (l, 0, j)),
        out_shape=jax.ShapeDtypeStruct((depth, MOD_ROWS, n), F32),
        compiler_params=_params("parallel", "parallel"),
        name="modulation",
    )(c_rows, w_mod, b_mod.reshape(depth, 1, n))


def _ln_mod_mm_kernel(x_ref, sc_ref, sh_ref, w_ref, o_ref):
    h = _layer_norm(x_ref[...]) * (1.0 + sc_ref[0]) + sh_ref[0]
    o_ref[...] = _dot(h.astype(BF16), w_ref[...]).astype(o_ref.dtype)


class _Tokens:
    def __init__(self, batch, seq, ctx_len):
        self.batch, self.seq, self.ctx_len = batch, seq, ctx_len
        self.n_lat = batch * seq
        self.n_all = self.n_lat + batch * ctx_len

    def mod_row(self, tm):
        n_lat_blocks = self.n_lat // tm
        per_seq = self.seq // tm
        batch = self.batch
        return lambda i: jnp.where(i < n_lat_blocks, i // per_seq, batch)


def _ln_mod_matmul(tok, x_all, n_tok, sc, sh, w, tm, tn, out_dtype=F32):
    d, n = w.shape
    row = tok.mod_row(tm)
    return pl.pallas_call(
        _ln_mod_mm_kernel,
        grid=(n // tn, n_tok // tm),
        in_specs=[
            pl.BlockSpec((tm, d), lambda j, i: (i, 0)),
            pl.BlockSpec((1, 1, d), lambda j, i: (row(i), 0, 0)),
            pl.BlockSpec((1, 1, d), lambda j, i: (row(i), 0, 0)),
            pl.BlockSpec((d, tn), lambda j, i: (0, j)),
        ],
        out_specs=pl.BlockSpec((tm, tn), lambda j, i: (i, j)),
        out_shape=jax.ShapeDtypeStruct((n_tok, n), out_dtype),
        compiler_params=_params("parallel", "parallel"),
        name="ln_mod_matmul",
    )(x_all, sc, sh, w)


def _gmlp_kernel(uv_ref, gate_ref, lng_ref, lnb_ref, ws_ref, bs_ref, wp_ref, o_ref, ha_ref, *, tm):
    u = jax.nn.gelu(uv_ref[:, :A_WIDTH])
    v = _layer_norm(jax.nn.gelu(uv_ref[:, A_WIDTH:])) * lng_ref[...] + lnb_ref[...]
    vb = v.astype(BF16)
    gw = A_WIDTH // A_GROUPS
    for ch in range(tm // CHUNK):
        r0 = ch * CHUNK
        for g in range(A_GROUPS):
            c0 = g * gw
            s = _dot(ws_ref[g], vb[r0:r0 + CHUNK, c0:c0 + gw])
            s = s + jnp.broadcast_to(bs_ref[:, g:g + 1], (CHUNK, gw))
            ha_ref[r0:r0 + CHUNK, c0:c0 + gw] = (u[r0:r0 + CHUNK, c0:c0 + gw] * s).astype(BF16)
    o_ref[...] = jax.nn.sigmoid(gate_ref[...]) * _dot(ha_ref[...], wp_ref[...])


def _gmlp_branch(uv, gates, ln_g, ln_b, ws, bs_t, wp, n_tok, tm):
    d = D_MODEL
    return pl.pallas_call(
        functools.partial(_gmlp_kernel, tm=tm),
        grid=(n_tok // tm,),
        in_specs=[
            pl.BlockSpec((tm, 2 * A_WIDTH), lambda i: (i, 0)),
            pl.BlockSpec((tm, d), lambda i: (i, 0)),
            pl.BlockSpec((1, A_WIDTH), lambda i: (0, 0)),
            pl.BlockSpec((1, A_WIDTH), lambda i: (0, 0)),
            pl.BlockSpec((A_GROUPS, CHUNK, CHUNK), lambda i: (0, 0, 0)),
            pl.BlockSpec((CHUNK, A_GROUPS), lambda i: (0, 0)),
            pl.BlockSpec((A_WIDTH, d), lambda i: (0, 0)),
        ],
        out_specs=pl.BlockSpec((tm, d), lambda i: (i, 0)),
        out_shape=jax.ShapeDtypeStruct((n_tok, d), F32),
        scratch_shapes=[pltpu.VMEM((tm, A_WIDTH), BF16)],
        compiler_params=_params("parallel"),
        name="gmlp_branch",
    )(uv, gates, ln_g, ln_b, ws, bs_t, wp)


def _conf_kernel(cur_ref, prev_ref, next_ref, gate_ref, w_ref, cb_ref, lng_ref, lnb_ref, wp_ref,
                 o_ref, z_ref, *, tm, n_lat_blocks, per_seq_lat, per_seq_ctx):
    i = pl.program_id(0)
    is_ctx = i >= n_lat_blocks
    pos = jnp.where(is_ctx, (i - n_lat_blocks) % per_seq_ctx, i % per_seq_lat)
    last = jnp.where(is_ctx, per_seq_ctx - 1, per_seq_lat - 1)

    def glu(ref):
        return ref[:, :C_WIDTH] * jax.nn.sigmoid(ref[:, C_WIDTH:])

    z_ref[0:CONV_HALO, :] = jnp.where(pos > 0, glu(prev_ref), 0.0)
    z_ref[CONV_HALO:CONV_HALO + tm, :] = glu(cur_ref)
    z_ref[CONV_HALO + tm:, :] = jnp.where(pos < last, glu(next_ref), 0.0)
    acc = jnp.zeros((tm, C_WIDTH), F32) + cb_ref[...]
    for k in range(CONV_W):
        off = CONV_HALO - CONV_W // 2 + k
        acc = acc + z_ref[off:off + tm, :] * w_ref[k:k + 1, :]
    y = _layer_norm(acc) * lng_ref[...] + lnb_ref[...]
    y = y * jax.nn.sigmoid(y)
    o_ref[...] = jax.nn.sigmoid(gate_ref[...]) * _dot(y.astype(BF16), wp_ref[...])


def _conf_branch(tok, glu, gates, conv_w, conv_b, ln_g, ln_b, wp, n_tok, tm):
    d = D_MODEL
    hb = tm // CONV_HALO
    n_halo_blocks = n_tok // CONV_HALO
    kernel = functools.partial(_conf_kernel, tm=tm, n_lat_blocks=tok.n_lat // tm,
                               per_seq_lat=tok.seq // tm, per_seq_ctx=tok.ctx_len // tm)
    return pl.pallas_call(
        kernel,
        grid=(n_tok // tm,),
        in_specs=[
            pl.BlockSpec((tm, 2 * C_WIDTH), lambda i: (i, 0)),
            pl.BlockSpec((CONV_HALO, 2 * C_WIDTH), lambda i: (jnp.maximum(i * hb - 1, 0), 0)),
            pl.BlockSpec((CONV_HALO, 2 * C_WIDTH),
                         lambda i: (jnp.minimum((i + 1) * hb, n_halo_blocks - 1), 0)),
            pl.BlockSpec((tm, d), lambda i: (i, 2)),
            pl.BlockSpec((CONV_W + 1, C_WIDTH), lambda i: (0, 0)),
            pl.BlockSpec((1, C_WIDTH), lambda i: (0, 0)),
            pl.BlockSpec((1, C_WIDTH), lambda i: (0, 0)),
            pl.BlockSpec((1, C_WIDTH), lambda i: (0, 0)),
            pl.BlockSpec((C_WIDTH, d), lambda i: (0, 0)),
        ],
        out_specs=pl.BlockSpec((tm, d), lambda i: (i, 0)),
        out_shape=jax.ShapeDtypeStruct((n_tok, d), F32),
        scratch_shapes=[pltpu.VMEM((tm + 2 * CONV_HALO, C_WIDTH), F32)],
        compiler_params=_params("parallel"),
        name="conformer_branch",
    )(glu, glu, glu, gates, conv_w, conv_b, ln_g, ln_b, wp)


def _qkv_prep_kernel(qkv_ref, cos_ref, sin_ref, bd_ref, gq_ref, gk_ref, q_ref, k_ref, v_ref, *, tm):
    lane = lax.broadcasted_iota(jnp.int32, (tm, LANES), 1)
    first_half = (lane % HEAD_DIM) < (HEAD_DIM // 2)
    low_head = lane < HEAD_DIM
    cos = cos_ref[...]
    sin = sin_ref[...]
    bd = bd_ref[...]

    def norm_rope(x, g):
        ms = _dot((x * x).astype(BF16), bd)
        xn = x * lax.rsqrt(ms + EPS) * g
        swapped = jnp.where(first_half, pltpu.roll(xn, LANES - HEAD_DIM // 2, 1),
                            pltpu.roll(xn, HEAD_DIM // 2, 1))
        return xn * cos + swapped * sin

    def head_masked(y, base, ref):
        rolled = pltpu.roll(y, HEAD_DIM, 1)
        zero = jnp.zeros_like(y)
        ref[:, base:base + LANES] = jnp.where(low_head, y, zero).astype(BF16)
        ref[:, base + LANES:base + 2 * LANES] = jnp.where(low_head, zero, rolled).astype(BF16)
        ref[:, base + 2 * LANES:base + 3 * LANES] = jnp.where(low_head, rolled, zero).astype(BF16)
        ref[:, base + 3 * LANES:base + 4 * LANES] = jnp.where(low_head, zero, y).astype(BF16)

    scale = HEAD_DIM ** -0.5
    for c in range(Q_DIM // LANES):
        y = norm_rope(qkv_ref[:, c * LANES:(c + 1) * LANES], gq_ref[...])
        q_ref[:, c * LANES:(c + 1) * LANES] = (y * scale).astype(BF16)
    for c in range(KV_DIM // LANES):
        y = norm_rope(qkv_ref[:, Q_DIM + c * LANES:Q_DIM + (c + 1) * LANES], gk_ref[...])
        head_masked(y, c * 4 * LANES, k_ref)
        head_masked(qkv_ref[:, Q_DIM + KV_DIM + c * LANES:Q_DIM + KV_DIM + (c + 1) * LANES],
                    c * 4 * LANES, v_ref)


def _qkv_prep(tok, qkv, cos_t, sin_t, bd, gq, gk, tm):
    n_tok = qkv.shape[0]
    n_lat_blocks = tok.n_lat // tm
    per_seq = tok.seq // tm
    ext = N_KV_HEADS * 2 * LANES

    def tab(i):
        return (jnp.where(i < n_lat_blocks, i % per_seq, per_seq), 0)

    return pl.pallas_call(
        functools.partial(_qkv_prep_kernel, tm=tm),
        grid=(n_tok // tm,),
        in_specs=[
            pl.BlockSpec((tm, Q_DIM + 2 * KV_DIM), lambda i: (i, 0)),
            pl.BlockSpec((tm, LANES), tab),
            pl.BlockSpec((tm, LANES), tab),
            pl.BlockSpec((LANES, LANES), lambda i: (0, 0)),
            pl.BlockSpec((1, LANES), lambda i: (0, 0)),
            pl.BlockSpec((1, LANES), lambda i: (0, 0)),
        ],
        out_specs=[
            pl.BlockSpec((tm, Q_DIM), lambda i: (i, 0)),
            pl.BlockSpec((tm, ext), lambda i: (i, 0)),
            pl.BlockSpec((tm, ext), lambda i: (i, 0)),
        ],
        out_shape=[
            jax.ShapeDtypeStruct((n_tok, Q_DIM), BF16),
            jax.ShapeDtypeStruct((n_tok, ext), BF16),
            jax.ShapeDtypeStruct((n_tok, ext), BF16),
        ],
        compiler_params=_params("parallel"),
        name="qkv_prep",
    )(qkv, cos_t, sin_t, bd, gq, gk)


def _attn_kernel(q_ref, kc_ref, vc_ref, *rest, with_latent):
    if with_latent:
        kl_ref, vl_ref, o_ref = rest
    else:
        (o_ref,) = rest
    for pair in range(N_HEADS // 2):
        qc = q_ref[:, pair * LANES:(pair + 1) * LANES]
        out = None
        for par in range(2):
            head = 2 * pair + par
            col = (head // (N_HEADS // N_KV_HEADS)) * 2 * LANES + par * LANES
            s_c = _dot_nt(qc, kc_ref[:, col:col + LANES])
            m = jnp.max(s_c, -1, keepdims=True)
            if with_latent:
                s_l = _dot_nt(qc, kl_ref[:, col:col + LANES])
                m = jnp.maximum(m, jnp.max(s_l, -1, keepdims=True))
            p_c = jnp.exp(s_c - m)
            denom = jnp.sum(p_c, -1, keepdims=True)
            o = _dot(p_c.astype(BF16), vc_ref[:, col:col + LANES])
            if with_latent:
                p_l = jnp.exp(s_l - m)
                denom = denom + jnp.sum(p_l, -1, keepdims=True)
                o = o + _dot(p_l.astype(BF16), vl_ref[:, col:col + LANES])
            o = o / denom
            out = o if out is None else out + o
        o_ref[:, pair * LANES:(pair + 1) * LANES] = out.astype(o_ref.dtype)


def _attention(tok, q, k_ext, v_ext, latent, tq):
    batch, seq, ctx_len = tok.batch, tok.seq, tok.ctx_len
    ext = k_ext.shape[1]
    ctx_block0 = tok.n_lat // ctx_len
    if latent:
        per = seq // tq
        grid = (batch, per)
        q_map = lambda b, i: (b * per + i, 0)
        n_out = tok.n_lat
        out_map = q_map
    else:
        per = ctx_len // tq
        q0 = tok.n_lat // tq
        grid = (batch, per)
        q_map = lambda b, i: (q0 + b * per + i, 0)
        n_out = batch * ctx_len
        out_map = lambda b, i: (b * per + i, 0)
    in_specs = [
        pl.BlockSpec((tq, Q_DIM), q_map),
        pl.BlockSpec((ctx_len, ext), lambda b, i: (ctx_block0 + b, 0)),
        pl.BlockSpec((ctx_len, ext), lambda b, i: (ctx_block0 + b, 0)),
    ]
    args = [q, k_ext, v_ext]
    if latent:
        in_specs += [pl.BlockSpec((seq, ext), lambda b, i: (b, 0)),
                     pl.BlockSpec((seq, ext), lambda b, i: (b, 0))]
        args += [k_ext, v_ext]
    return pl.pallas_call(
        functools.partial(_attn_kernel, with_latent=latent),
        grid=grid,
        in_specs=in_specs,
        out_specs=pl.BlockSpec((tq, Q_DIM), out_map),
        out_shape=jax.ShapeDtypeStruct((n_out, Q_DIM), BF16),
        compiler_params=_params("parallel", "parallel"),
        name="attention_latent" if latent else "attention_context",
    )(*args)


def _merge_kernel(x_ref, a_ref, att_ref, c_ref, gate_ref, g1_ref, bwp_ref, wout_ref, lng_ref,
                  lnb_ref, o_ref, *, alpha):
    m = a_ref[...] + c_ref[...] + jax.nn.sigmoid(gate_ref[...]) * _dot(att_ref[...], bwp_ref[...])
    y = _dot(m.astype(BF16), wout_ref[...])
    z = alpha * x_ref[...] + g1_ref[0] * y
    o_ref[...] = _layer_norm(z) * lng_ref[...] + lnb_ref[...]


def _merge(tok, x_all, ha, att, hc, gates, g1, b_wp, w_out, ln_g, ln_b, n_tok, tm, alpha):
    d = D_MODEL
    row = tok.mod_row(tm)
    tok_spec = pl.BlockSpec((tm, d), lambda i: (i, 0))
    vec_spec = pl.BlockSpec((1, d), lambda i: (0, 0))
    mat_spec = pl.BlockSpec((d, d), lambda i: (0, 0))
    return pl.pallas_call(
        functools.partial(_merge_kernel, alpha=alpha),
        grid=(n_tok // tm,),
        in_specs=[tok_spec, tok_spec, tok_spec, tok_spec,
                  pl.BlockSpec((tm, d), lambda i: (i, 1)),
                  pl.BlockSpec((1, 1, d), lambda i: (row(i), 0, 0)),
                  mat_spec, mat_spec, vec_spec, vec_spec],
        out_specs=tok_spec,
        out_shape=jax.ShapeDtypeStruct((n_tok, d), F32),
        compiler_params=_params("parallel"),
        name="merge",
    )(x_all, ha, att, hc, gates, g1, b_wp, w_out, ln_g, ln_b)


def _peer_query_kernel(x_ref, sc_ref, sh_ref, wq_ref, keys_ref, h2t_ref, st_ref):
    h2 = _layer_norm(x_ref[...]) * (1.0 + sc_ref[0]) + sh_ref[0]
    h2t_ref[...] = h2.T.astype(BF16)
    q = _dot(h2.astype(BF16), wq_ref[...])
    for hp in range(2 * PEER_HEADS):
        qhp = q[:, hp * PEER_HALF:(hp + 1) * PEER_HALF].astype(BF16)
        st_ref[hp] = _dot_nt(keys_ref[hp % 2], qhp)


def _peer_query(tok, x_all, sc, sh, wq, keys, n_tok, tm):
    d = D_MODEL
    row = tok.mod_row(tm)
    return pl.pallas_call(
        _peer_query_kernel,
        grid=(n_tok // tm,),
        in_specs=[
            pl.BlockSpec((tm, d), lambda i: (i, 0)),
            pl.BlockSpec((1, 1, d), lambda i: (row(i), 0, 0)),
            pl.BlockSpec((1, 1, d), lambda i: (row(i), 0, 0)),
            pl.BlockSpec((d, PEER_HEADS * PEER_QDIM), lambda i: (0, 0)),
            pl.BlockSpec((2, N_KEYS, PEER_HALF), lambda i: (0, 0, 0)),
        ],
        out_specs=[
            pl.BlockSpec((d, tm), lambda i: (0, i)),
            pl.BlockSpec((2 * PEER_HEADS, N_KEYS, tm), lambda i: (0, 0, i)),
        ],
        out_shape=[
            jax.ShapeDtypeStruct((d, n_tok), BF16),
            jax.ShapeDtypeStruct((2 * PEER_HEADS, N_KEYS, n_tok), F32),
        ],
        compiler_params=_params("parallel"),
        name="peer_query",
    )(x_all, sc, sh, wq, keys)


def _next_float(x, up):
    bits = pltpu.bitcast(x + 0.0, jnp.int32)
    away = 1 if up else -1
    step = jnp.where(x >= 0.0, away, -away)
    return pltpu.bitcast(bits + step, F32)


def _peer_topk_kernel(st_ref, b_ref, c_ref, r_ref, top_ref):
    neg = -jnp.inf

    def head(h, carry):
        for p in range(2):
            cur = st_ref[2 * h + p]
            for k in range(PEER_TOPK):
                m = jnp.max(cur, axis=0, keepdims=True)
                top_ref[p, k:k + 1, :] = m
                cur = jnp.where(cur >= m, neg, cur)
        v1 = top_ref[0]
        v2 = top_ref[1]
        pieces = [v1[0:1] + v2]
        for a in range(1, 8):
            pieces.append(v1[a:a + 1] + v2[0:8])
        pieces.append(v1[8:16] + v2[0:1])
        cand = jnp.concatenate(pieces, axis=0)
        m0 = jnp.max(cand, axis=0, keepdims=True)
        cand = jnp.where(cand >= m0, neg, cand)
        zsum = jnp.ones_like(m0)
        m = m0
        for k in range(1, PEER_TOPK):
            m = jnp.max(cand, axis=0, keepdims=True)
            zsum = zsum + jnp.exp(m - m0)
            cand = jnp.where(cand >= m, neg, cand)
        thr = m
        s1 = st_ref[2 * h]
        s2 = st_ref[2 * h + 1]
        c_ref[h] = jnp.exp(s1 - v1[0:1])
        r_ref[h] = jnp.exp(s2 - v2[0:1]) / zsum
        b = thr - s1
        for _ in range(2):
            b = jnp.where(s1 + b >= thr, b, _next_float(b, True))
        for _ in range(2):
            lower = _next_float(b, False)
            b = jnp.where(s1 + lower >= thr, lower, b)
        b_ref[h] = b
        return carry

    lax.fori_loop(0, PEER_HEADS, head, 0)


def _peer_topk(st, tl):
    n_tok = st.shape[2]
    spec = pl.BlockSpec((PEER_HEADS, N_KEYS, tl), lambda i: (0, 0, i))
    shape = jax.ShapeDtypeStruct((PEER_HEADS, N_KEYS, n_tok), F32)
    return pl.pallas_call(
        _peer_topk_kernel,
        grid=(n_tok // tl,),
        in_specs=[pl.BlockSpec((2 * PEER_HEADS, N_KEYS, tl), lambda i: (0, 0, i))],
        out_specs=[spec, spec, spec],
        out_shape=[shape, shape, shape],
        scratch_shapes=[pltpu.VMEM((2, PEER_TOPK, tl), F32)],
        compiler_params=_params("parallel"),
        name="peer_topk",
    )(st)


def _peer_main_kernel(h2t_ref, u_ref, vt_ref, s2_ref, r_ref, b_ref, c_ref, x_ref, g2_ref, lng_ref,
                      lnb_ref, o_ref, acc_ref, w_ref, *, n_slab, alpha):
    k = pl.program_id(1)

    @pl.when(k == 0)
    def _():
        acc_ref[...] = jnp.zeros_like(acc_ref)

    st = _dot(u_ref[...], h2t_ref[...])
    for sl in range(n_slab):
        g = None
        for h in range(PEER_HEADS):
            picked = s2_ref[h, 0] >= b_ref[h, sl:sl + 1, :]
            term = jnp.where(picked, c_ref[h, sl:sl + 1, :] * r_ref[h], 0.0)
            g = term if g is None else g + term
        w = jax.nn.gelu(st[sl * N_KEYS:(sl + 1) * N_KEYS, :]) * g
        w_ref[sl * N_KEYS:(sl + 1) * N_KEYS, :] = w.astype(BF16)
    acc_ref[...] += _dot(vt_ref[...], w_ref[...])

    @pl.when(k == pl.num_programs(1) - 1)
    def _():
        z = alpha * x_ref[...] + g2_ref[0] * acc_ref[...].T
        o_ref[...] = _layer_norm(z) * lng_ref[...] + lnb_ref[...]


def _peer_main(tok, x_all, h2t, st, b, c, r, g2, u_tab, vt_tab, ln_g, ln_b, n_tok, tb, n_slab, alpha):
    d = D_MODEL
    ec = n_slab * N_KEYS
    row = tok.mod_row(tb)
    st4 = st.reshape(PEER_HEADS, 2, N_KEYS, st.shape[2])
    return pl.pallas_call(
        functools.partial(_peer_main_kernel, n_slab=n_slab, alpha=alpha),
        grid=(n_tok // tb, N_EXPERTS // ec),
        in_specs=[
            pl.BlockSpec((d, tb), lambda t, k: (0, t)),
            pl.BlockSpec((ec, d), lambda t, k: (k, 0)),
            pl.BlockSpec((d, ec), lambda t, k: (0, k)),
            pl.BlockSpec((PEER_HEADS, 1, N_KEYS, tb), lambda t, k: (0, 1, 0, t)),
            pl.BlockSpec((PEER_HEADS, N_KEYS, tb), lambda t, k: (0, 0, t)),
            pl.BlockSpec((PEER_HEADS, n_slab, tb), lambda t, k: (0, k, t)),
            pl.BlockSpec((PEER_HEADS, n_slab, tb), lambda t, k: (0, k, t)),
            pl.BlockSpec((tb, d), lambda t, k: (t, 0)),
            pl.BlockSpec((1, 1, d), lambda t, k: (row(t), 0, 0)),
            pl.BlockSpec((1, d), lambda t, k: (0, 0)),
            pl.BlockSpec((1, d), lambda t, k: (0, 0)),
        ],
        out_specs=pl.BlockSpec((tb, d), lambda t, k: (t, 0)),
        out_shape=jax.ShapeDtypeStruct((n_tok, d), F32),
        scratch_shapes=[pltpu.VMEM((d, tb), F32), pltpu.VMEM((ec, tb), BF16)],
        compiler_params=_params("parallel", "arbitrary"),
        name="peer_main",
    )(h2t, u_tab, vt_tab, st4, r, b, c, x_all, g2, ln_g, ln_b)


def _rope_tables(seq, tm):
    rows = seq // GRID_W
    row = jnp.repeat(jnp.arange(rows, dtype=F32), GRID_W)
    col = jnp.tile(jnp.arange(GRID_W, dtype=F32), rows)
    n_freq = HEAD_DIM // 4
    inv = ROPE_THETA ** (-jnp.arange(n_freq, dtype=F32) / n_freq)
    ang = jnp.concatenate([row[:, None] * inv, col[:, None] * inv], -1)
    cos, sin = jnp.cos(ang), jnp.sin(ang)
    cos_t = jnp.tile(jnp.concatenate([cos, cos], -1), (1, LANES // HEAD_DIM))
    sin_t = jnp.tile(jnp.concatenate([-sin, sin], -1), (1, LANES // HEAD_DIM))
    cos_t = jnp.concatenate([cos_t, jnp.ones((tm, LANES), F32)], 0)
    sin_t = jnp.concatenate([sin_t, jnp.zeros((tm, LANES), F32)], 0)
    return cos_t, sin_t


def _row(v):
    return v.reshape(1, -1)


def kernel(x, c, ctx, c_ctx, w_mod, b_mod, w_in, a_ln_g, a_ln_b, a_ws, a_bs, a_wp, q_norm_g, k_norm_g, b_wp, c_conv_w, c_conv_b, c_ln_g, c_ln_b, c_wp, w_out, ln1_g, ln1_b, peer_wq, peer_keys, peer_u, peer_v, ln2_g, ln2_b):
    batch, seq, d = x.shape
    ctx_len = ctx.shape[1]
    depth = w_mod.shape[0]
    assert d == D_MODEL and batch < MOD_ROWS
    tok = _Tokens(batch, seq, ctx_len)
    alpha = float((2 * depth) ** 0.25)

    tm = min(256, ctx_len)
    tq = min(256, ctx_len)
    tb = min(512, batch * ctx_len)
    n_slab = 8
    assert seq % tm == 0 and ctx_len % tm == 0 and seq % tb == 0 and (batch * ctx_len) % tb == 0

    x_all = jnp.concatenate([x.reshape(batch * seq, d), ctx.reshape(batch * ctx_len, d)], 0)
    c_rows = jnp.concatenate([c, c_ctx[None], jnp.zeros((MOD_ROWS - batch - 1, d), F32)], 0)
    mod = _modulation(c_rows, w_mod, b_mod)

    cos_t, sin_t = _rope_tables(seq, tm)
    head_perm = np.concatenate([np.arange(0, HEAD_DIM, 2), np.arange(1, HEAD_DIM, 2)])
    q_cols = 2 * A_WIDTH + (np.arange(N_HEADS)[:, None] * HEAD_DIM + head_perm[None]).reshape(-1)
    k_cols = 2 * A_WIDTH + Q_DIM + (np.arange(N_KV_HEADS)[:, None] * HEAD_DIM + head_perm[None]).reshape(-1)
    v_cols = 2 * A_WIDTH + Q_DIM + KV_DIM + np.arange(KV_DIM)
    qkv_cols = np.concatenate([q_cols, k_cols, v_cols])
    o_glu = 2 * A_WIDTH + Q_DIM + 2 * KV_DIM
    o_gate = o_glu + 2 * C_WIDTH
    lane_head = np.arange(LANES) // HEAD_DIM
    bd = jnp.asarray((lane_head[:, None] == lane_head[None]) / HEAD_DIM, BF16)

    for i in range(depth):
        last = i == depth - 1
        n_mix = tok.n_lat if last else tok.n_all

        def mod_chunk(j, i=i):
            return mod[i, :, j * d:(j + 1) * d].reshape(MOD_ROWS, 1, d)

        sh1, sc1, g1, sh2, sc2, g2 = (mod_chunk(j) for j in range(6))
        w_i = w_in[i]
        w_uv = w_i[:, :2 * A_WIDTH].astype(BF16)
        w_qkv = w_i[:, qkv_cols].astype(BF16)
        w_glu = w_i[:, o_glu:o_gate].astype(BF16)
        w_gate = w_i[:, o_gate:].astype(BF16)

        uv = _ln_mod_matmul(tok, x_all, n_mix, sc1, sh1, w_uv, tm, 2 * A_WIDTH)
        qkv = _ln_mod_matmul(tok, x_all, tok.n_all, sc1, sh1, w_qkv, tm, Q_DIM + 2 * KV_DIM)
        glu = _ln_mod_matmul(tok, x_all, n_mix, sc1, sh1, w_glu, tm, 2 * C_WIDTH)
        gates = _ln_mod_matmul(tok, x_all, n_mix, sc1, sh1, w_gate, tm, 3 * d)

        ha = _gmlp_branch(uv, gates, _row(a_ln_g[i]), _row(a_ln_b[i]), a_ws[i].astype(BF16),
                          a_bs[i].T, a_wp[i].astype(BF16), n_mix, tm)
        conv_w = jnp.concatenate([c_conv_w[i], jnp.zeros((1, C_WIDTH), F32)], 0)
        hc = _conf_branch(tok, glu, gates, conv_w, _row(c_conv_b[i]), _row(c_ln_g[i]),
                          _row(c_ln_b[i]), c_wp[i].astype(BF16), n_mix, tm)
        gq = jnp.tile(q_norm_g[i][head_perm], LANES // HEAD_DIM).reshape(1, LANES)
        gk = jnp.tile(k_norm_g[i][head_perm], LANES // HEAD_DIM).reshape(1, LANES)
        q_r, k_ext, v_ext = _qkv_prep(tok, qkv, cos_t, sin_t, bd, gq, gk, tm)
        att = _attention(tok, q_r, k_ext, v_ext, True, tq)
        if not last:
            att = jnp.concatenate([att, _attention(tok, q_r, k_ext, v_ext, False, tq)], 0)
        x_mix = _merge(tok, x_all, ha, att, hc, gates, g1, b_wp[i].astype(BF16),
                       w_out[i].astype(BF16), _row(ln1_g[i]), _row(ln1_b[i]), n_mix, tm, alpha)

        h2t, st = _peer_query(tok, x_mix, sc2, sh2, peer_wq[i].astype(BF16),
                              peer_keys[i].astype(BF16), n_mix, tm)
        b_thr, c_fac, r_fac = _peer_topk(st, tm)
        x_new = _peer_main(tok, x_mix, h2t, st, b_thr, c_fac, r_fac, g2, peer_u[i].astype(BF16),
                           peer_v[i].T.astype(BF16), _row(ln2_g[i]), _row(ln2_b[i]), n_mix, tb,
                           n_slab, alpha)
        if last:
            return x_new.reshape(batch, seq, d)
        x_all = x_new
```

```python
import functools

import jax
import jax.numpy as jnp
import numpy as np
from jax import lax
from jax.experimental import pallas as pl
from jax.experimental.pallas import tpu as pltpu

D_MODEL = 1024
GRID_W = 64
CHUNK = 128
A_WIDTH = 1024
A_GROUPS = 4
N_HEADS = 16
N_KV_HEADS = 4
HEAD_DIM = 64
Q_DIM = N_HEADS * HEAD_DIM
KV_DIM = N_KV_HEADS * HEAD_DIM
ROPE_THETA = 10000.0
C_WIDTH = 1024
CONV_W = 31
CONV_HALO = 16
PEER_HEADS = 8
N_KEYS = 128
N_EXPERTS = N_KEYS * N_KEYS
PEER_TOPK = 16
PEER_QDIM = 256
PEER_HALF = PEER_QDIM // 2
EPS = 1e-6
MOD_ROWS = 16
LANES = 128
BF16_ROWS = 16
GELU_K1 = float(np.sqrt(2.0 / np.pi))
GELU_K2 = float(np.sqrt(2.0 / np.pi) * 0.044715)

BF16 = jnp.bfloat16
F32 = jnp.float32

VMEM_LIMIT_BYTES = 56 * 1024 * 1024


def _params(*sem):
    return pltpu.CompilerParams(dimension_semantics=sem, vmem_limit_bytes=VMEM_LIMIT_BYTES)


def _layer_norm(x):
    mu = jnp.mean(x, -1, keepdims=True)
    xc = x - mu
    var = jnp.mean(xc * xc, -1, keepdims=True)
    return xc * lax.rsqrt(var + EPS)


def _split_bf16(a):
    hi = a.astype(BF16)
    lo = (a - hi.astype(F32)).astype(BF16)
    return hi, lo


def _dot(a, b):
    return jnp.dot(a, b, preferred_element_type=F32)


def _dot_nt(a, b):
    return lax.dot_general(a, b, (((1,), (1,)), ((), ())), preferred_element_type=F32)


def _mod_kernel(c_ref, w_ref, b_ref, o_ref):
    c = c_ref[...]
    a_hi, a_lo = _split_bf16(c * jax.nn.sigmoid(c))
    w_hi, w_lo = _split_bf16(w_ref[0])
    o_ref[0] = _dot(a_hi, w_hi) + _dot(a_hi, w_lo) + _dot(a_lo, w_hi) + b_ref[0]


def _modulation(c_rows, w_mod, b_mod):
    depth, d, n = w_mod.shape
    tn = 1536
    return pl.pallas_call(
        _mod_kernel,
        grid=(depth, n // tn),
        in_specs=[
            pl.BlockSpec((MOD_ROWS, d), lambda l, j: (0, 0)),
            pl.BlockSpec((1, d, tn), lambda l, j: (l, 0, j)),
            pl.BlockSpec((1, 1, tn), lambda l, j: (l, 0, j)),
        ],
        out_specs=pl.BlockSpec((1, MOD_ROWS, tn), lambda l, j: (l, 0, j)),
        out_shape=jax.ShapeDtypeStruct((depth, MOD_ROWS, n), F32),
        compiler_params=_params("parallel", "parallel"),
        name="modulation",
    )(c_rows, w_mod, b_mod.reshape(depth, 1, n))


def _ln_mod_mm_kernel(x_ref, sc_ref, sh_ref, w_ref, o_ref):
    h = _layer_norm(x_ref[...]) * (1.0 + sc_ref[0]) + sh_ref[0]
    o_ref[...] = _dot(h.astype(BF16), w_ref[...]).astype(o_ref.dtype)


class _Tokens:
    def __init__(self, batch, seq, ctx_len):
        self.batch, self.seq, self.ctx_len = batch, seq, ctx_len
        self.n_lat = batch * seq
        self.n_all = self.n_lat + batch * ctx_len

    def mod_row(self, tm):
        n_lat_blocks = self.n_lat // tm
        per_seq = self.seq // tm
        batch = self.batch
        return lambda i: jnp.where(i < n_lat_blocks, i // per_seq, batch)


def _ln_mod_matmul(tok, x_all, n_tok, sc, sh, w, tm, tn, out_dtype=F32):
    d, n = w.shape
    row = tok.mod_row(tm)
    return pl.pallas_call(
        _ln_mod_mm_kernel,
        grid=(n // tn, n_tok // tm),
        in_specs=[
            pl.BlockSpec((tm, d), lambda j, i: (i, 0)),
            pl.BlockSpec((1, 1, d), lambda j, i: (row(i), 0, 0)),
            pl.BlockSpec((1, 1, d), lambda j, i: (row(i), 0, 0)),
            pl.BlockSpec((d, tn), lambda j, i: (0, j)),
        ],
        out_specs=pl.BlockSpec((tm, tn), lambda j, i: (i, j)),
        out_shape=jax.ShapeDtypeStruct((n_tok, n), out_dtype),
        compiler_params=_params("parallel", "parallel"),
        name="ln_mod_matmul",
    )(x_all, sc, sh, w)


def _gmlp_kernel(uv_ref, gate_ref, lng_ref, lnb_ref, ws_ref, bs_ref, wp_ref, o_ref, ha_ref, *, tm):
    u = jax.nn.gelu(uv_ref[:, :A_WIDTH])
    v = _layer_norm(jax.nn.gelu(uv_ref[:, A_WIDTH:])) * lng_ref[...] + lnb_ref[...]
    vb = v.astype(BF16)
    gw = A_WIDTH // A_GROUPS
    for ch in range(tm // CHUNK):
        r0 = ch * CHUNK
        for g in range(A_GROUPS):
            c0 = g * gw
            s = _dot(ws_ref[g], vb[r0:r0 + CHUNK, c0:c0 + gw])
            s = s + jnp.broadcast_to(bs_ref[:, g:g + 1], (CHUNK, gw))
            ha_ref[r0:r0 + CHUNK, c0:c0 + gw] = (u[r0:r0 + CHUNK, c0:c0 + gw] * s).astype(BF16)
    o_ref[...] = jax.nn.sigmoid(gate_ref[...]) * _dot(ha_ref[...], wp_ref[...])


def _gmlp_branch(uv, gates, ln_g, ln_b, ws, bs_t, wp, n_tok, tm):
    d = D_MODEL
    return pl.pallas_call(
        functools.partial(_gmlp_kernel, tm=tm),
        grid=(n_tok // tm,),
        in_specs=[
            pl.BlockSpec((tm, 2 * A_WIDTH), lambda i: (i, 0)),
            pl.BlockSpec((tm, d), lambda i: (i, 0)),
            pl.BlockSpec((1, A_WIDTH), lambda i: (0, 0)),
            pl.BlockSpec((1, A_WIDTH), lambda i: (0, 0)),
            pl.BlockSpec((A_GROUPS, CHUNK, CHUNK), lambda i: (0, 0, 0)),
            pl.BlockSpec((CHUNK, A_GROUPS), lambda i: (0, 0)),
            pl.BlockSpec((A_WIDTH, d), lambda i: (0, 0)),
        ],
        out_specs=pl.BlockSpec((tm, d), lambda i: (i, 0)),
        out_shape=jax.ShapeDtypeStruct((n_tok, d), F32),
        scratch_shapes=[pltpu.VMEM((tm, A_WIDTH), BF16)],
        compiler_params=_params("parallel"),
        name="gmlp_branch",
    )(uv, gates, ln_g, ln_b, ws, bs_t, wp)


def _conf_kernel(cur_ref, prev_ref, next_ref, gate_ref, w_ref, cb_ref, lng_ref, lnb_ref, wp_ref,
                 o_ref, z_ref, *, tm, n_lat_blocks, per_seq_lat, per_seq_ctx):
    i = pl.program_id(0)
    is_ctx = i >= n_lat_blocks
    pos = jnp.where(is_ctx, (i - n_lat_blocks) % per_seq_ctx, i % per_seq_lat)
    last = jnp.where(is_ctx, per_seq_ctx - 1, per_seq_lat - 1)

    def glu(ref):
        return ref[:, :C_WIDTH] * jax.nn.sigmoid(ref[:, C_WIDTH:])

    z_ref[0:CONV_HALO, :] = jnp.where(pos > 0, glu(prev_ref), 0.0)
    z_ref[CONV_HALO:CONV_HALO + tm, :] = glu(cur_ref)
    z_ref[CONV_HALO + tm:, :] = jnp.where(pos < last, glu(next_ref), 0.0)
    acc = jnp.zeros((tm, C_WIDTH), F32) + cb_ref[...]
    for k in range(CONV_W):
        off = CONV_HALO - CONV_W // 2 + k
        acc = acc + z_ref[off:off + tm, :] * w_ref[k:k + 1, :]
    y = _layer_norm(acc) * lng_ref[...] + lnb_ref[...]
    y = y * jax.nn.sigmoid(y)
    o_ref[...] = jax.nn.sigmoid(gate_ref[...]) * _dot(y.astype(BF16), wp_ref[...])


def _conf_branch(tok, glu, gates, conv_w, conv_b, ln_g, ln_b, wp, n_tok, tm):
    d = D_MODEL
    hb = tm // CONV_HALO
    n_halo_blocks = n_tok // CONV_HALO
    kernel = functools.partial(_conf_kernel, tm=tm, n_lat_blocks=tok.n_lat // tm,
                               per_seq_lat=tok.seq // tm, per_seq_ctx=tok.ctx_len // tm)
    return pl.pallas_call(
        kernel,
        grid=(n_tok // tm,),
        in_specs=[
            pl.BlockSpec((tm, 2 * C_WIDTH), lambda i: (i, 0)),
            pl.BlockSpec((CONV_HALO, 2 * C_WIDTH), lambda i: (jnp.maximum(i * hb - 1, 0), 0)),
            pl.BlockSpec((CONV_HALO, 2 * C_WIDTH),
                         lambda i: (jnp.minimum((i + 1) * hb, n_halo_blocks - 1), 0)),
            pl.BlockSpec((tm, d), lambda i: (i, 2)),
            pl.BlockSpec((CONV_W + 1, C_WIDTH), lambda i: (0, 0)),
            pl.BlockSpec((1, C_WIDTH), lambda i: (0, 0)),
            pl.BlockSpec((1, C_WIDTH), lambda i: (0, 0)),
            pl.BlockSpec((1, C_WIDTH), lambda i: (0, 0)),
            pl.BlockSpec((C_WIDTH, d), lambda i: (0, 0)),
        ],
        out_specs=pl.BlockSpec((tm, d), lambda i: (i, 0)),
        out_shape=jax.ShapeDtypeStruct((n_tok, d), F32),
        scratch_shapes=[pltpu.VMEM((tm + 2 * CONV_HALO, C_WIDTH), F32)],
        compiler_params=_params("parallel"),
        name="conformer_branch",
    )(glu, glu, glu, gates, conv_w, conv_b, ln_g, ln_b, wp)


def _qkv_prep_kernel(qkv_ref, cos_ref, sin_ref, bd_ref, gq_ref, gk_ref, q_ref, k_ref, v_ref, *, tm):
    lane = lax.broadcasted_iota(jnp.int32, (tm, LANES), 1)
    first_half = (lane % HEAD_DIM) < (HEAD_DIM // 2)
    low_head = lane < HEAD_DIM
    cos = cos_ref[...]
    sin = sin_ref[...]
    bd = bd_ref[...]

    def norm_rope(x, g):
        ms = _dot((x * x).astype(BF16), bd)
        xn = x * lax.rsqrt(ms + EPS) * g
        swapped = jnp.where(first_half, pltpu.roll(xn, LANES - HEAD_DIM // 2, 1),
                            pltpu.roll(xn, HEAD_DIM // 2, 1))
        return xn * cos + swapped * sin

    def head_masked(y, base, ref):
        rolled = pltpu.roll(y, HEAD_DIM, 1)
        zero = jnp.zeros_like(y)
        ref[:, base:base + LANES] = jnp.where(low_head, y, zero).astype(BF16)
        ref[:, base + LANES:base + 2 * LANES] = jnp.where(low_head, zero, rolled).astype(BF16)
        ref[:, base + 2 * LANES:base + 3 * LANES] = jnp.where(low_head, rolled, zero).astype(BF16)
        ref[:, base + 3 * LANES:base + 4 * LANES] = jnp.where(low_head, zero, y).astype(BF16)

    scale = HEAD_DIM ** -0.5
    for c in range(Q_DIM // LANES):
        y = norm_rope(qkv_ref[:, c * LANES:(c + 1) * LANES], gq_ref[...])
        q_ref[:, c * LANES:(c + 1) * LANES] = (y * scale).astype(BF16)
    for c in range(KV_DIM // LANES):
        y = norm_rope(qkv_ref[:, Q_DIM + c * LANES:Q_DIM + (c + 1) * LANES], gk_ref[...])
        head_masked(y, c * 4 * LANES, k_ref)
        head_masked(qkv_ref[:, Q_DIM + KV_DIM + c * LANES:Q_DIM + KV_DIM + (c + 1) * LANES],
                    c * 4 * LANES, v_ref)


def _qkv_prep(tok, qkv, cos_t, sin_t, bd, gq, gk, tm):
    n_tok = qkv.shape[0]
    n_lat_blocks = tok.n_lat // tm
    per_seq = tok.seq // tm
    ext = N_KV_HEADS * 2 * LANES

    def tab(i):
        return (jnp.where(i < n_lat_blocks, i % per_seq, per_seq), 0)

    return pl.pallas_call(
        functools.partial(_qkv_prep_kernel, tm=tm),
        grid=(n_tok // tm,),
        in_specs=[
            pl.BlockSpec((tm, Q_DIM + 2 * KV_DIM), lambda i: (i, 0)),
            pl.BlockSpec((tm, LANES), tab),
            pl.BlockSpec((tm, LANES), tab),
            pl.BlockSpec((LANES, LANES), lambda i: (0, 0)),
            pl.BlockSpec((1, LANES), lambda i: (0, 0)),
            pl.BlockSpec((1, LANES), lambda i: (0, 0)),
        ],
        out_specs=[
            pl.BlockSpec((tm, Q_DIM), lambda i: (i, 0)),
            pl.BlockSpec((tm, ext), lambda i: (i, 0)),
            pl.BlockSpec((tm, ext), lambda i: (i, 0)),
        ],
        out_shape=[
            jax.ShapeDtypeStruct((n_tok, Q_DIM), BF16),
            jax.ShapeDtypeStruct((n_tok, ext), BF16),
            jax.ShapeDtypeStruct((n_tok, ext), BF16),
        ],
        compiler_params=_params("parallel"),
        name="qkv_prep",
    )(qkv, cos_t, sin_t, bd, gq, gk)


def _attn_kernel(q_ref, kc_ref, vc_ref, *rest, with_latent):
    if with_latent:
        kl_ref, vl_ref, o_ref = rest
    else:
        (o_ref,) = rest
    for pair in range(N_HEADS // 2):
        qc = q_ref[:, pair * LANES:(pair + 1) * LANES]
        out = None
        for par in range(2):
            head = 2 * pair + par
            col = (head // (N_HEADS // N_KV_HEADS)) * 2 * LANES + par * LANES
            s_c = _dot_nt(qc, kc_ref[:, col:col + LANES])
            m = jnp.max(s_c, -1, keepdims=True)
            if with_latent:
                s_l = _dot_nt(qc, kl_ref[:, col:col + LANES])
                m = jnp.maximum(m, jnp.max(s_l, -1, keepdims=True))
            p_c = jnp.exp(s_c - m)
            denom = jnp.sum(p_c, -1, keepdims=True)
            o = _dot(p_c.astype(BF16), vc_ref[:, col:col + LANES])
            if with_latent:
                p_l = jnp.exp(s_l - m)
                denom = denom + jnp.sum(p_l, -1, keepdims=True)
                o = o + _dot(p_l.astype(BF16), vl_ref[:, col:col + LANES])
            o = o / denom
            out = o if out is None else out + o
        o_ref[:, pair * LANES:(pair + 1) * LANES] = out.astype(o_ref.dtype)


def _attention(tok, q, k_ext, v_ext, latent, tq):
    batch, seq, ctx_len = tok.batch, tok.seq, tok.ctx_len
    ext = k_ext.shape[1]
    ctx_block0 = tok.n_lat // ctx_len
    if latent:
        per = seq // tq
        grid = (batch, per)
        q_map = lambda b, i: (b * per + i, 0)
        n_out = tok.n_lat
        out_map = q_map
    else:
        per = ctx_len // tq
        q0 = tok.n_lat // tq
        grid = (batch, per)
        q_map = lambda b, i: (q0 + b * per + i, 0)
        n_out = batch * ctx_len
        out_map = lambda b, i: (b * per + i, 0)
    in_specs = [
        pl.BlockSpec((tq, Q_DIM), q_map),
        pl.BlockSpec((ctx_len, ext), lambda b, i: (ctx_block0 + b, 0)),
        pl.BlockSpec((ctx_len, ext), lambda b, i: (ctx_block0 + b, 0)),
    ]
    args = [q, k_ext, v_ext]
    if latent:
        in_specs += [pl.BlockSpec((seq, ext), lambda b, i: (b, 0)),
                     pl.BlockSpec((seq, ext), lambda b, i: (b, 0))]
        args += [k_ext, v_ext]
    return pl.pallas_call(
        functools.partial(_attn_kernel, with_latent=latent),
        grid=grid,
        in_specs=in_specs,
        out_specs=pl.BlockSpec((tq, Q_DIM), out_map),
        out_shape=jax.ShapeDtypeStruct((n_out, Q_DIM), BF16),
        compiler_params=_params("parallel", "parallel"),
        name="attention_latent" if latent else "attention_context",
    )(*args)


def _merge_kernel(x_ref, a_ref, att_ref, c_ref, gate_ref, g1_ref, bwp_ref, wout_ref, lng_ref,
                  lnb_ref, o_ref, *, alpha):
    m = a_ref[...] + c_ref[...] + jax.nn.sigmoid(gate_ref[...]) * _dot(att_ref[...], bwp_ref[...])
    y = _dot(m.astype(BF16), wout_ref[...])
    z = alpha * x_ref[...] + g1_ref[0] * y
    o_ref[...] = _layer_norm(z) * lng_ref[...] + lnb_ref[...]


def _merge(tok, x_all, ha, att, hc, gates, g1, b_wp, w_out, ln_g, ln_b, n_tok, tm, alpha):
    d = D_MODEL
    row = tok.mod_row(tm)
    tok_spec = pl.BlockSpec((tm, d), lambda i: (i, 0))
    vec_spec = pl.BlockSpec((1, d), lambda i: (0, 0))
    mat_spec = pl.BlockSpec((d, d), lambda i: (0, 0))
    return pl.pallas_call(
        functools.partial(_merge_kernel, alpha=alpha),
        grid=(n_tok // tm,),
        in_specs=[tok_spec, tok_spec, tok_spec, tok_spec,
                  pl.BlockSpec((tm, d), lambda i: (i, 1)),
                  pl.BlockSpec((1, 1, d), lambda i: (row(i), 0, 0)),
                  mat_spec, mat_spec, vec_spec, vec_spec],
        out_specs=tok_spec,
        out_shape=jax.ShapeDtypeStruct((n_tok, d), F32),
        compiler_params=_params("parallel"),
        name="merge",
    )(x_all, ha, att, hc, gates, g1, b_wp, w_out, ln_g, ln_b)


def _peer_query_kernel(x_ref, sc_ref, sh_ref, wq_ref, keys_ref, h2t_ref, st_ref, *, tm):
    h2 = _layer_norm(x_ref[...]) * (1.0 + sc_ref[0]) + sh_ref[0]
    h2t_ref[...] = h2.T.astype(BF16)
    q = _dot(h2.astype(BF16), wq_ref[...])
    for hp in range(2 * PEER_HEADS):
        qhp = q[:, hp * PEER_HALF:(hp + 1) * PEER_HALF].astype(BF16)
        st = _dot_nt(keys_ref[hp % 2], qhp)
        for lt in range(tm // LANES):
            st_ref[hp, lt] = st[:, lt * LANES:(lt + 1) * LANES]


def _peer_query(tok, x_all, sc, sh, wq, keys, n_tok, tm):
    d = D_MODEL
    row = tok.mod_row(tm)
    return pl.pallas_call(
        functools.partial(_peer_query_kernel, tm=tm),
        grid=(n_tok // tm,),
        in_specs=[
            pl.BlockSpec((tm, d), lambda i: (i, 0)),
            pl.BlockSpec((1, 1, d), lambda i: (row(i), 0, 0)),
            pl.BlockSpec((1, 1, d), lambda i: (row(i), 0, 0)),
            pl.BlockSpec((d, PEER_HEADS * PEER_QDIM), lambda i: (0, 0)),
            pl.BlockSpec((2, N_KEYS, PEER_HALF), lambda i: (0, 0, 0)),
        ],
        out_specs=[
            pl.BlockSpec((d, tm), lambda i: (0, i)),
            pl.BlockSpec((2 * PEER_HEADS, tm // LANES, N_KEYS, LANES), lambda i: (0, i, 0, 0)),
        ],
        out_shape=[
            jax.ShapeDtypeStruct((d, n_tok), BF16),
            jax.ShapeDtypeStruct((2 * PEER_HEADS, n_tok // LANES, N_KEYS, LANES), F32),
        ],
        compiler_params=_params("parallel"),
        name="peer_query",
    )(x_all, sc, sh, wq, keys)


def _peer_topk_kernel(st_ref, n_ref, c_ref, rank_ref, r_ref, top_ref, *, n_tiles):
    neg = -jnp.inf

    def tile(idx, carry):
        h = idx // n_tiles
        lt = idx % n_tiles
        s1 = st_ref[2 * h, lt]
        s2 = st_ref[2 * h + 1, lt]
        cur = s1
        for k in range(PEER_TOPK):
            m = jnp.max(cur, axis=0, keepdims=True)
            top_ref[0, k:k + 1, :] = m
            cur = jnp.where(cur >= m, neg, cur)
        cur = s2
        rank = jnp.full(s2.shape, float(N_KEYS), F32)
        for k in range(PEER_TOPK):
            m = jnp.max(cur, axis=0, keepdims=True)
            top_ref[1, k:k + 1, :] = m
            hit = cur >= m
            rank = jnp.where(hit, float(k), rank)
            cur = jnp.where(hit, neg, cur)
        v1 = top_ref[0]
        v2 = top_ref[1]
        pieces = [v1[0:1] + v2]
        for a in range(1, 8):
            pieces.append(v1[a:a + 1] + v2[0:8])
        pieces.append(v1[8:16] + v2[0:1])
        cand = jnp.concatenate(pieces, axis=0)
        m0 = jnp.max(cand, axis=0, keepdims=True)
        cand = jnp.where(cand >= m0, neg, cand)
        zsum = jnp.ones_like(m0)
        m = m0
        for k in range(1, PEER_TOPK):
            m = jnp.max(cand, axis=0, keepdims=True)
            zsum = zsum + jnp.exp(m - m0)
            cand = jnp.where(cand >= m, neg, cand)
        thr = m
        cnt = jnp.zeros(s1.shape, F32)
        for k in range(PEER_TOPK):
            cnt = cnt + jnp.where(s1 + v2[k:k + 1] >= thr, 1.0, 0.0)
        lanes = pl.ds(pl.multiple_of(lt * LANES, LANES), LANES)
        n_ref[h, :, lanes] = cnt
        c_ref[h, :, lanes] = jnp.exp(s1 - v1[0:1])
        rank_ref[h, lt] = rank.astype(BF16)
        r_ref[h, lt] = (jnp.exp(s2 - v2[0:1]) * (0.5 / zsum)).astype(BF16)
        return carry

    lax.fori_loop(0, PEER_HEADS * n_tiles, tile, 0)


def _peer_topk(st, tl):
    n_tiles_all = st.shape[1]
    n_tok = n_tiles_all * LANES
    n_tiles = tl // LANES
    row_spec = pl.BlockSpec((PEER_HEADS, N_KEYS, tl), lambda i: (0, 0, i))
    row_shape = jax.ShapeDtypeStruct((PEER_HEADS, N_KEYS, n_tok), F32)
    tile_spec = pl.BlockSpec((PEER_HEADS, n_tiles, N_KEYS, LANES), lambda i: (0, i, 0, 0))
    tile_shape = jax.ShapeDtypeStruct((PEER_HEADS, n_tiles_all, N_KEYS, LANES), BF16)
    return pl.pallas_call(
        functools.partial(_peer_topk_kernel, n_tiles=n_tiles),
        grid=(n_tok // tl,),
        in_specs=[pl.BlockSpec((2 * PEER_HEADS, n_tiles, N_KEYS, LANES), lambda i: (0, i, 0, 0))],
        out_specs=[row_spec, row_spec, tile_spec, tile_spec],
        out_shape=[row_shape, row_shape, tile_shape, tile_shape],
        scratch_shapes=[pltpu.VMEM((2, PEER_TOPK, LANES), F32)],
        compiler_params=_params("parallel"),
        name="peer_topk",
    )(st)


def _peer_main_kernel(h2t_ref, u_ref, vt_ref, rank_ref, r_ref, n_ref, c_ref, x_ref, g2_ref, lng_ref,
                      lnb_ref, o_ref, acc_ref, w_ref, st_ref, *, n_slab, tb, alpha):
    k = pl.program_id(1)

    @pl.when(k == 0)
    def _():
        acc_ref[...] = jnp.zeros_like(acc_ref)

    st = _dot(u_ref[...], h2t_ref[...])
    for lt in range(tb // LANES):
        st_ref[lt] = st[:, lt * LANES:(lt + 1) * LANES]

    def row_bf16(ref, h, sl, lanes):
        return jnp.broadcast_to(ref[h, sl:sl + 1, lanes], (BF16_ROWS, LANES)).astype(BF16)

    def tile(lt, carry, *, sl):
        lanes = slice(lt * LANES, (lt + 1) * LANES)
        groups = [slice(r0, r0 + BF16_ROWS) for r0 in range(0, N_KEYS, BF16_ROWS)]
        g = [None] * len(groups)
        for h in range(PEER_HEADS):
            cnt = row_bf16(n_ref, h, sl, lanes)
            cfac = row_bf16(c_ref, h, sl, lanes)
            for gi, rows in enumerate(groups):
                picked_c = jnp.minimum(jnp.maximum(cnt - rank_ref[h, lt, rows, :], 0.0), cfac)
                term = picked_c * r_ref[h, lt, rows, :]
                g[gi] = term if g[gi] is None else g[gi] + term
        for gi, rows in enumerate(groups):
            e0 = sl * N_KEYS + rows.start
            x = st_ref[lt, e0:e0 + BF16_ROWS, :]
            t = jnp.tanh(x * (GELU_K1 + GELU_K2 * (x * x)))
            w_ref[e0:e0 + BF16_ROWS, lanes] = (x * (1.0 + t)).astype(BF16) * g[gi]
        return carry

    for sl in range(n_slab):
        for lt in range(tb // LANES):
            tile(lt, 0, sl=sl)
    acc_ref[...] += _dot(vt_ref[...], w_ref[...])

    @pl.when(k == pl.num_programs(1) - 1)
    def _():
        z = alpha * x_ref[...] + g2_ref[0] * acc_ref[...].T
        o_ref[...] = _layer_norm(z) * lng_ref[...] + lnb_ref[...]


def _peer_main(tok, x_all, h2t, n_cnt, c_fac, rank, r_fac, g2, u_tab, vt_tab, ln_g, ln_b, n_tok, tb,
               n_slab, alpha):
    d = D_MODEL
    ec = n_slab * N_KEYS
    row = tok.mod_row(tb)
    tile_spec = pl.BlockSpec((PEER_HEADS, tb // LANES, N_KEYS, LANES), lambda t, k: (0, t, 0, 0))
    return pl.pallas_call(
        functools.partial(_peer_main_kernel, n_slab=n_slab, tb=tb, alpha=alpha),
        grid=(n_tok // tb, N_EXPERTS // ec),
        in_specs=[
            pl.BlockSpec((d, tb), lambda t, k: (0, t)),
            pl.BlockSpec((ec, d), lambda t, k: (k, 0)),
            pl.BlockSpec((d, ec), lambda t, k: (0, k)),
            tile_spec,
            tile_spec,
            pl.BlockSpec((PEER_HEADS, n_slab, tb), lambda t, k: (0, k, t)),
            pl.BlockSpec((PEER_HEADS, n_slab, tb), lambda t, k: (0, k, t)),
            pl.BlockSpec((tb, d), lambda t, k: (t, 0)),
            pl.BlockSpec((1, 1, d), lambda t, k: (row(t), 0, 0)),
            pl.BlockSpec((1, d), lambda t, k: (0, 0)),
            pl.BlockSpec((1, d), lambda t, k: (0, 0)),
        ],
        out_specs=pl.BlockSpec((tb, d), lambda t, k: (t, 0)),
        out_shape=jax.ShapeDtypeStruct((n_tok, d), F32),
        scratch_shapes=[pltpu.VMEM((d, tb), F32), pltpu.VMEM((ec, tb), BF16),
                        pltpu.VMEM((tb // LANES, ec, LANES), F32)],
        compiler_params=_params("parallel", "arbitrary"),
        name="peer_main",
    )(h2t, u_tab, vt_tab, rank, r_fac, n_cnt, c_fac, x_all, g2, ln_g, ln_b)


def _rope_tables(seq, tm):
    rows = seq // GRID_W
    row = jnp.repeat(jnp.arange(rows, dtype=F32), GRID_W)
    col = jnp.tile(jnp.arange(GRID_W, dtype=F32), rows)
    n_freq = HEAD_DIM // 4
    inv = ROPE_THETA ** (-jnp.arange(n_freq, dtype=F32) / n_freq)
    ang = jnp.concatenate([row[:, None] * inv, col[:, None] * inv], -1)
    cos, sin = jnp.cos(ang), jnp.sin(ang)
    cos_t = jnp.tile(jnp.concatenate([cos, cos], -1), (1, LANES // HEAD_DIM))
    sin_t = jnp.tile(jnp.concatenate([-sin, sin], -1), (1, LANES // HEAD_DIM))
    cos_t = jnp.concatenate([cos_t, jnp.ones((tm, LANES), F32)], 0)
    sin_t = jnp.concatenate([sin_t, jnp.zeros((tm, LANES), F32)], 0)
    return cos_t, sin_t


def _row(v):
    return v.reshape(1, -1)


def kernel(x, c, ctx, c_ctx, w_mod, b_mod, w_in, a_ln_g, a_ln_b, a_ws, a_bs, a_wp, q_norm_g, k_norm_g, b_wp, c_conv_w, c_conv_b, c_ln_g, c_ln_b, c_wp, w_out, ln1_g, ln1_b, peer_wq, peer_keys, peer_u, peer_v, ln2_g, ln2_b):
    batch, seq, d = x.shape
    ctx_len = ctx.shape[1]
    depth = w_mod.shape[0]
    assert d == D_MODEL and batch < MOD_ROWS
    tok = _Tokens(batch, seq, ctx_len)
    alpha = float((2 * depth) ** 0.25)

    tm = min(256, ctx_len)
    tq = min(256, ctx_len)
    tb = min(512, batch * ctx_len)
    n_slab = 8
    assert seq % tm == 0 and ctx_len % tm == 0 and seq % tb == 0 and (batch * ctx_len) % tb == 0

    x_all = jnp.concatenate([x.reshape(batch * seq, d), ctx.reshape(batch * ctx_len, d)], 0)
    c_rows = jnp.concatenate([c, c_ctx[None], jnp.zeros((MOD_ROWS - batch - 1, d), F32)], 0)
    mod = _modulation(c_rows, w_mod, b_mod)

    cos_t, sin_t = _rope_tables(seq, tm)
    head_perm = np.concatenate([np.arange(0, HEAD_DIM, 2), np.arange(1, HEAD_DIM, 2)])
    q_cols = 2 * A_WIDTH + (np.arange(N_HEADS)[:, None] * HEAD_DIM + head_perm[None]).reshape(-1)
    k_cols = 2 * A_WIDTH + Q_DIM + (np.arange(N_KV_HEADS)[:, None] * HEAD_DIM + head_perm[None]).reshape(-1)
    v_cols = 2 * A_WIDTH + Q_DIM + KV_DIM + np.arange(KV_DIM)
    qkv_cols = np.concatenate([q_cols, k_cols, v_cols])
    o_glu = 2 * A_WIDTH + Q_DIM + 2 * KV_DIM
    o_gate = o_glu + 2 * C_WIDTH
    lane_head = np.arange(LANES) // HEAD_DIM
    bd = jnp.asarray((lane_head[:, None] == lane_head[None]) / HEAD_DIM, BF16)

    for i in range(depth):
        last = i == depth - 1
        n_mix = tok.n_lat if last else tok.n_all

        def mod_chunk(j, i=i):
            return mod[i, :, j * d:(j + 1) * d].reshape(MOD_ROWS, 1, d)

        sh1, sc1, g1, sh2, sc2, g2 = (mod_chunk(j) for j in range(6))
        w_i = w_in[i]
        w_uv = w_i[:, :2 * A_WIDTH].astype(BF16)
        w_qkv = w_i[:, qkv_cols].astype(BF16)
        w_glu = w_i[:, o_glu:o_gate].astype(BF16)
        w_gate = w_i[:, o_gate:].astype(BF16)

        uv = _ln_mod_matmul(tok, x_all, n_mix, sc1, sh1, w_uv, tm, 2 * A_WIDTH)
        qkv = _ln_mod_matmul(tok, x_all, tok.n_all, sc1, sh1, w_qkv, tm, Q_DIM + 2 * KV_DIM)
        glu = _ln_mod_matmul(tok, x_all, n_mix, sc1, sh1, w_glu, tm, 2 * C_WIDTH)
        gates = _ln_mod_matmul(tok, x_all, n_mix, sc1, sh1, w_gate, tm, 3 * d)

        ha = _gmlp_branch(uv, gates, _row(a_ln_g[i]), _row(a_ln_b[i]), a_ws[i].astype(BF16),
                          a_bs[i].T, a_wp[i].astype(BF16), n_mix, tm)
        conv_w = jnp.concatenate([c_conv_w[i], jnp.zeros((1, C_WIDTH), F32)], 0)
        hc = _conf_branch(tok, glu, gates, conv_w, _row(c_conv_b[i]), _row(c_ln_g[i]),
                          _row(c_ln_b[i]), c_wp[i].astype(BF16), n_mix, tm)
        gq = jnp.tile(q_norm_g[i][head_perm], LANES // HEAD_DIM).reshape(1, LANES)
        gk = jnp.tile(k_norm_g[i][head_perm], LANES // HEAD_DIM).reshape(1, LANES)
        q_r, k_ext, v_ext = _qkv_prep(tok, qkv, cos_t, sin_t, bd, gq, gk, tm)
        att = _attention(tok, q_r, k_ext, v_ext, True, tq)
        if not last:
            att = jnp.concatenate([att, _attention(tok, q_r, k_ext, v_ext, False, tq)], 0)
        x_mix = _merge(tok, x_all, ha, att, hc, gates, g1, b_wp[i].astype(BF16),
                       w_out[i].astype(BF16), _row(ln1_g[i]), _row(ln1_b[i]), n_mix, tm, alpha)

        h2t, st = _peer_query(tok, x_mix, sc2, sh2, peer_wq[i].astype(BF16),
                              peer_keys[i].astype(BF16), n_mix, tm)
        n_cnt, c_fac, rank, r_fac = _peer_topk(st, tm)
        x_new = _peer_main(tok, x_mix, h2t, n_cnt, c_fac, rank, r_fac, g2, peer_u[i].astype(BF16),
                           peer_v[i].T.astype(BF16), _row(ln2_g[i]), _row(ln2_b[i]), n_mix, tb,
                           n_slab, alpha)
        if last:
            return x_new.reshape(batch, seq, d)
        x_all = x_new
```

```python
import functools

import jax
import jax.numpy as jnp
import numpy as np
from jax import lax
from jax.experimental import pallas as pl
from jax.experimental.pallas import tpu as pltpu

D_MODEL = 1024
GRID_W = 64
CHUNK = 128
A_WIDTH = 1024
A_GROUPS = 4
N_HEADS = 16
N_KV_HEADS = 4
HEAD_DIM = 64
Q_DIM = N_HEADS * HEAD_DIM
KV_DIM = N_KV_HEADS * HEAD_DIM
ROPE_THETA = 10000.0
C_WIDTH = 1024
CONV_W = 31
CONV_HALO = 16
PEER_HEADS = 8
N_KEYS = 128
N_EXPERTS = N_KEYS * N_KEYS
PEER_TOPK = 16
PEER_QDIM = 256
PEER_HALF = PEER_QDIM // 2
EPS = 1e-6
MOD_ROWS = 16
LANES = 128
SUBLANES = 8
GELU_K1 = float(np.sqrt(2.0 / np.pi))
GELU_K2 = float(np.sqrt(2.0 / np.pi) * 0.044715)

BF16 = jnp.bfloat16
F32 = jnp.float32

VMEM_LIMIT_BYTES = 56 * 1024 * 1024


def _params(*sem):
    return pltpu.CompilerParams(dimension_semantics=sem, vmem_limit_bytes=VMEM_LIMIT_BYTES)


def _layer_norm(x):
    mu = jnp.mean(x, -1, keepdims=True)
    xc = x - mu
    var = jnp.mean(xc * xc, -1, keepdims=True)
    return xc * lax.rsqrt(var + EPS)


def _split_bf16(a):
    hi = a.astype(BF16)
    lo = (a - hi.astype(F32)).astype(BF16)
    return hi, lo


def _dot(a, b):
    return jnp.dot(a, b, preferred_element_type=F32)


def _dot_nt(a, b):
    return lax.dot_general(a, b, (((1,), (1,)), ((), ())), preferred_element_type=F32)


def _mod_kernel(c_ref, w_ref, b_ref, o_ref):
    c = c_ref[...]
    a_hi, a_lo = _split_bf16(c * jax.nn.sigmoid(c))
    w_hi, w_lo = _split_bf16(w_ref[0])
    o_ref[0] = _dot(a_hi, w_hi) + _dot(a_hi, w_lo) + _dot(a_lo, w_hi) + b_ref[0]


def _modulation(c_rows, w_mod, b_mod):
    depth, d, n = w_mod.shape
    tn = 1536
    return pl.pallas_call(
        _mod_kernel,
        grid=(depth, n // tn),
        in_specs=[
            pl.BlockSpec((MOD_ROWS, d), lambda l, j: (0, 0)),
            pl.BlockSpec((1, d, tn), lambda l, j: (l, 0, j)),
            pl.BlockSpec((1, 1, tn), lambda l, j: (l, 0, j)),
        ],
        out_specs=pl.BlockSpec((1, MOD_ROWS, tn), lambda l, j: (l, 0, j)),
        out_shape=jax.ShapeDtypeStruct((depth, MOD_ROWS, n), F32),
        compiler_params=_params("parallel", "parallel"),
        name="modulation",
    )(c_rows, w_mod, b_mod.reshape(depth, 1, n))


def _ln_mod_mm_kernel(x_ref, sc_ref, sh_ref, w_ref, o_ref):
    h = _layer_norm(x_ref[...]) * (1.0 + sc_ref[0]) + sh_ref[0]
    o_ref[...] = _dot(h.astype(BF16), w_ref[...]).astype(o_ref.dtype)


class _Tokens:
    def __init__(self, batch, seq, ctx_len):
        self.batch, self.seq, self.ctx_len = batch, seq, ctx_len
        self.n_lat = batch * seq
        self.n_all = self.n_lat + batch * ctx_len

    def mod_row(self, tm):
        n_lat_blocks = self.n_lat // tm
        per_seq = self.seq // tm
        batch = self.batch
        return lambda i: jnp.where(i < n_lat_blocks, i // per_seq, batch)


def _ln_mod_matmul(tok, x_all, n_tok, sc, sh, w, tm, tn, out_dtype=F32):
    d, n = w.shape
    row = tok.mod_row(tm)
    return pl.pallas_call(
        _ln_mod_mm_kernel,
        grid=(n // tn, n_tok // tm),
        in_specs=[
            pl.BlockSpec((tm, d), lambda j, i: (i, 0)),
            pl.BlockSpec((1, 1, d), lambda j, i: (row(i), 0, 0)),
            pl.BlockSpec((1, 1, d), lambda j, i: (row(i), 0, 0)),
            pl.BlockSpec((d, tn), lambda j, i: (0, j)),
        ],
        out_specs=pl.BlockSpec((tm, tn), lambda j, i: (i, j)),
        out_shape=jax.ShapeDtypeStruct((n_tok, n), out_dtype),
        compiler_params=_params("parallel", "parallel"),
        name="ln_mod_matmul",
    )(x_all, sc, sh, w)


def _gmlp_kernel(uv_ref, gate_ref, lng_ref, lnb_ref, ws_ref, bs_ref, wp_ref, o_ref, ha_ref, *, tm):
    u = jax.nn.gelu(uv_ref[:, :A_WIDTH])
    v = _layer_norm(jax.nn.gelu(uv_ref[:, A_WIDTH:])) * lng_ref[...] + lnb_ref[...]
    vb = v.astype(BF16)
    gw = A_WIDTH // A_GROUPS
    for ch in range(tm // CHUNK):
        r0 = ch * CHUNK
        for g in range(A_GROUPS):
            c0 = g * gw
            s = _dot(ws_ref[g], vb[r0:r0 + CHUNK, c0:c0 + gw])
            s = s + jnp.broadcast_to(bs_ref[:, g:g + 1], (CHUNK, gw))
            ha_ref[r0:r0 + CHUNK, c0:c0 + gw] = (u[r0:r0 + CHUNK, c0:c0 + gw] * s).astype(BF16)
    o_ref[...] = jax.nn.sigmoid(gate_ref[...]) * _dot(ha_ref[...], wp_ref[...])


def _gmlp_branch(uv, gates, ln_g, ln_b, ws, bs_t, wp, n_tok, tm):
    d = D_MODEL
    return pl.pallas_call(
        functools.partial(_gmlp_kernel, tm=tm),
        grid=(n_tok // tm,),
        in_specs=[
            pl.BlockSpec((tm, 2 * A_WIDTH), lambda i: (i, 0)),
            pl.BlockSpec((tm, d), lambda i: (i, 0)),
            pl.BlockSpec((1, A_WIDTH), lambda i: (0, 0)),
            pl.BlockSpec((1, A_WIDTH), lambda i: (0, 0)),
            pl.BlockSpec((A_GROUPS, CHUNK, CHUNK), lambda i: (0, 0, 0)),
            pl.BlockSpec((CHUNK, A_GROUPS), lambda i: (0, 0)),
            pl.BlockSpec((A_WIDTH, d), lambda i: (0, 0)),
        ],
        out_specs=pl.BlockSpec((tm, d), lambda i: (i, 0)),
        out_shape=jax.ShapeDtypeStruct((n_tok, d), F32),
        scratch_shapes=[pltpu.VMEM((tm, A_WIDTH), BF16)],
        compiler_params=_params("parallel"),
        name="gmlp_branch",
    )(uv, gates, ln_g, ln_b, ws, bs_t, wp)


def _conf_kernel(cur_ref, prev_ref, next_ref, gate_ref, w_ref, cb_ref, lng_ref, lnb_ref, wp_ref,
                 o_ref, z_ref, zs_ref, *, tm, n_lat_blocks, per_seq_lat, per_seq_ctx):
    i = pl.program_id(0)
    is_ctx = i >= n_lat_blocks
    pos = jnp.where(is_ctx, (i - n_lat_blocks) % per_seq_ctx, i % per_seq_lat)
    last = jnp.where(is_ctx, per_seq_ctx - 1, per_seq_lat - 1)

    def glu(ref):
        return ref[:, :C_WIDTH] * jax.nn.sigmoid(ref[:, C_WIDTH:])

    z_ref[0:CONV_HALO, :] = jnp.where(pos > 0, glu(prev_ref), 0.0)
    z_ref[CONV_HALO:CONV_HALO + tm, :] = glu(cur_ref)
    z_ref[CONV_HALO + tm:, :] = jnp.where(pos < last, glu(next_ref), 0.0)
    acc = jnp.zeros((tm, C_WIDTH), F32) + cb_ref[...]
    first = CONV_HALO - CONV_W // 2
    span = tm + (CONV_W // SUBLANES) * SUBLANES
    for r in range(SUBLANES):
        zs_ref[r] = z_ref[r:r + span, :]
    for r in range(SUBLANES):
        for off in range(first, first + CONV_W):
            if off % SUBLANES == r:
                k = off - first
                acc = acc + zs_ref[r, off - r:off - r + tm, :] * w_ref[k:k + 1, :]
    y = _layer_norm(acc) * lng_ref[...] + lnb_ref[...]
    y = y * jax.nn.sigmoid(y)
    o_ref[...] = jax.nn.sigmoid(gate_ref[...]) * _dot(y.astype(BF16), wp_ref[...])


def _conf_branch(tok, glu, gates, conv_w, conv_b, ln_g, ln_b, wp, n_tok, tm):
    d = D_MODEL
    hb = tm // CONV_HALO
    n_halo_blocks = n_tok // CONV_HALO
    kernel = functools.partial(_conf_kernel, tm=tm, n_lat_blocks=tok.n_lat // tm,
                               per_seq_lat=tok.seq // tm, per_seq_ctx=tok.ctx_len // tm)
    return pl.pallas_call(
        kernel,
        grid=(n_tok // tm,),
        in_specs=[
            pl.BlockSpec((tm, 2 * C_WIDTH), lambda i: (i, 0)),
            pl.BlockSpec((CONV_HALO, 2 * C_WIDTH), lambda i: (jnp.maximum(i * hb - 1, 0), 0)),
            pl.BlockSpec((CONV_HALO, 2 * C_WIDTH),
                         lambda i: (jnp.minimum((i + 1) * hb, n_halo_blocks - 1), 0)),
            pl.BlockSpec((tm, d), lambda i: (i, 2)),
            pl.BlockSpec((CONV_W + 1, C_WIDTH), lambda i: (0, 0)),
            pl.BlockSpec((1, C_WIDTH), lambda i: (0, 0)),
            pl.BlockSpec((1, C_WIDTH), lambda i: (0, 0)),
            pl.BlockSpec((1, C_WIDTH), lambda i: (0, 0)),
            pl.BlockSpec((C_WIDTH, d), lambda i: (0, 0)),
        ],
        out_specs=pl.BlockSpec((tm, d), lambda i: (i, 0)),
        out_shape=jax.ShapeDtypeStruct((n_tok, d), F32),
        scratch_shapes=[pltpu.VMEM((tm + 2 * CONV_HALO, C_WIDTH), F32),
                        pltpu.VMEM((SUBLANES, tm + (CONV_W // SUBLANES) * SUBLANES, C_WIDTH), F32)],
        compiler_params=_params("parallel"),
        name="conformer_branch",
    )(glu, glu, glu, gates, conv_w, conv_b, ln_g, ln_b, wp)


def _qkv_prep_kernel(qkv_ref, cos_ref, sin_ref, bd_ref, gq_ref, gk_ref, q_ref, k_ref, v_ref, *, tm):
    lane = lax.broadcasted_iota(jnp.int32, (tm, LANES), 1)
    first_half = (lane % HEAD_DIM) < (HEAD_DIM // 2)
    low_head = lane < HEAD_DIM
    cos = cos_ref[...]
    sin = sin_ref[...]
    bd = bd_ref[...]

    def norm_rope(x, g):
        ms = _dot((x * x).astype(BF16), bd)
        xn = x * lax.rsqrt(ms + EPS) * g
        swapped = jnp.where(first_half, pltpu.roll(xn, LANES - HEAD_DIM // 2, 1),
                            pltpu.roll(xn, HEAD_DIM // 2, 1))
        return xn * cos + swapped * sin

    def head_masked(y, base, ref):
        rolled = pltpu.roll(y, HEAD_DIM, 1)
        zero = jnp.zeros_like(y)
        ref[:, base:base + LANES] = jnp.where(low_head, y, zero).astype(BF16)
        ref[:, base + LANES:base + 2 * LANES] = jnp.where(low_head, zero, rolled).astype(BF16)
        ref[:, base + 2 * LANES:base + 3 * LANES] = jnp.where(low_head, rolled, zero).astype(BF16)
        ref[:, base + 3 * LANES:base + 4 * LANES] = jnp.where(low_head, zero, y).astype(BF16)

    scale = HEAD_DIM ** -0.5
    for c in range(Q_DIM // LANES):
        y = norm_rope(qkv_ref[:, c * LANES:(c + 1) * LANES], gq_ref[...])
        q_ref[:, c * LANES:(c + 1) * LANES] = (y * scale).astype(BF16)
    for c in range(KV_DIM // LANES):
        y = norm_rope(qkv_ref[:, Q_DIM + c * LANES:Q_DIM + (c + 1) * LANES], gk_ref[...])
        head_masked(y, c * 4 * LANES, k_ref)
        head_masked(qkv_ref[:, Q_DIM + KV_DIM + c * LANES:Q_DIM + KV_DIM + (c + 1) * LANES],
                    c * 4 * LANES, v_ref)


def _qkv_prep(tok, qkv, cos_t, sin_t, bd, gq, gk, tm):
    n_tok = qkv.shape[0]
    n_lat_blocks = tok.n_lat // tm
    per_seq = tok.seq // tm
    ext = N_KV_HEADS * 2 * LANES

    def tab(i):
        return (jnp.where(i < n_lat_blocks, i % per_seq, per_seq), 0)

    return pl.pallas_call(
        functools.partial(_qkv_prep_kernel, tm=tm),
        grid=(n_tok // tm,),
        in_specs=[
            pl.BlockSpec((tm, Q_DIM + 2 * KV_DIM), lambda i: (i, 0)),
            pl.BlockSpec((tm, LANES), tab),
            pl.BlockSpec((tm, LANES), tab),
            pl.BlockSpec((LANES, LANES), lambda i: (0, 0)),
            pl.BlockSpec((1, LANES), lambda i: (0, 0)),
            pl.BlockSpec((1, LANES), lambda i: (0, 0)),
        ],
        out_specs=[
            pl.BlockSpec((tm, Q_DIM), lambda i: (i, 0)),
            pl.BlockSpec((tm, ext), lambda i: (i, 0)),
            pl.BlockSpec((tm, ext), lambda i: (i, 0)),
        ],
        out_shape=[
            jax.ShapeDtypeStruct((n_tok, Q_DIM), BF16),
            jax.ShapeDtypeStruct((n_tok, ext), BF16),
            jax.ShapeDtypeStruct((n_tok, ext), BF16),
        ],
        compiler_params=_params("parallel"),
        name="qkv_prep",
    )(qkv, cos_t, sin_t, bd, gq, gk)


def _attn_kernel(q_ref, kc_ref, vc_ref, *rest, with_latent):
    if with_latent:
        kl_ref, vl_ref, o_ref = rest
    else:
        (o_ref,) = rest
    for pair in range(N_HEADS // 2):
        qc = q_ref[:, pair * LANES:(pair + 1) * LANES]
        out = None
        for par in range(2):
            head = 2 * pair + par
            col = (head // (N_HEADS // N_KV_HEADS)) * 2 * LANES + par * LANES
            s_c = _dot_nt(qc, kc_ref[:, col:col + LANES])
            m = jnp.max(s_c, -1, keepdims=True)
            if with_latent:
                s_l = _dot_nt(qc, kl_ref[:, col:col + LANES])
                m = jnp.maximum(m, jnp.max(s_l, -1, keepdims=True))
            p_c = jnp.exp(s_c - m)
            denom = jnp.sum(p_c, -1, keepdims=True)
            o = _dot(p_c.astype(BF16), vc_ref[:, col:col + LANES])
            if with_latent:
                p_l = jnp.exp(s_l - m)
                denom = denom + jnp.sum(p_l, -1, keepdims=True)
                o = o + _dot(p_l.astype(BF16), vl_ref[:, col:col + LANES])
            o = o / denom
            out = o if out is None else out + o
        o_ref[:, pair * LANES:(pair + 1) * LANES] = out.astype(o_ref.dtype)


def _attention(tok, q, k_ext, v_ext, latent, tq):
    batch, seq, ctx_len = tok.batch, tok.seq, tok.ctx_len
    ext = k_ext.shape[1]
    ctx_block0 = tok.n_lat // ctx_len
    if latent:
        per = seq // tq
        grid = (batch, per)
        q_map = lambda b, i: (b * per + i, 0)
        n_out = tok.n_lat
        out_map = q_map
    else:
        per = ctx_len // tq
        q0 = tok.n_lat // tq
        grid = (batch, per)
        q_map = lambda b, i: (q0 + b * per + i, 0)
        n_out = batch * ctx_len
        out_map = lambda b, i: (b * per + i, 0)
    in_specs = [
        pl.BlockSpec((tq, Q_DIM), q_map),
        pl.BlockSpec((ctx_len, ext), lambda b, i: (ctx_block0 + b, 0)),
        pl.BlockSpec((ctx_len, ext), lambda b, i: (ctx_block0 + b, 0)),
    ]
    args = [q, k_ext, v_ext]
    if latent:
        in_specs += [pl.BlockSpec((seq, ext), lambda b, i: (b, 0)),
                     pl.BlockSpec((seq, ext), lambda b, i: (b, 0))]
        args += [k_ext, v_ext]
    return pl.pallas_call(
        functools.partial(_attn_kernel, with_latent=latent),
        grid=grid,
        in_specs=in_specs,
        out_specs=pl.BlockSpec((tq, Q_DIM), out_map),
        out_shape=jax.ShapeDtypeStruct((n_out, Q_DIM), BF16),
        compiler_params=_params("parallel", "parallel"),
        name="attention_latent" if latent else "attention_context",
    )(*args)


def _merge_kernel(x_ref, a_ref, att_ref, c_ref, gate_ref, g1_ref, bwp_ref, wout_ref, lng_ref,
                  lnb_ref, o_ref, *, alpha):
    m = a_ref[...] + c_ref[...] + jax.nn.sigmoid(gate_ref[...]) * _dot(att_ref[...], bwp_ref[...])
    y = _dot(m.astype(BF16), wout_ref[...])
    z = alpha * x_ref[...] + g1_ref[0] * y
    o_ref[...] = _layer_norm(z) * lng_ref[...] + lnb_ref[...]


def _merge(tok, x_all, ha, att, hc, gates, g1, b_wp, w_out, ln_g, ln_b, n_tok, tm, alpha):
    d = D_MODEL
    row = tok.mod_row(tm)
    tok_spec = pl.BlockSpec((tm, d), lambda i: (i, 0))
    vec_spec = pl.BlockSpec((1, d), lambda i: (0, 0))
    mat_spec = pl.BlockSpec((d, d), lambda i: (0, 0))
    return pl.pallas_call(
        functools.partial(_merge_kernel, alpha=alpha),
        grid=(n_tok // tm,),
        in_specs=[tok_spec, tok_spec, tok_spec, tok_spec,
                  pl.BlockSpec((tm, d), lambda i: (i, 1)),
                  pl.BlockSpec((1, 1, d), lambda i: (row(i), 0, 0)),
                  mat_spec, mat_spec, vec_spec, vec_spec],
        out_specs=tok_spec,
        out_shape=jax.ShapeDtypeStruct((n_tok, d), F32),
        compiler_params=_params("parallel"),
        name="merge",
    )(x_all, ha, att, hc, gates, g1, b_wp, w_out, ln_g, ln_b)


def _peer_query_kernel(x_ref, sc_ref, sh_ref, wq_ref, keys_ref, h2t_ref, st_ref, *, tm):
    h2 = _layer_norm(x_ref[...]) * (1.0 + sc_ref[0]) + sh_ref[0]
    h2t_ref[...] = h2.T.astype(BF16)
    q = _dot(h2.astype(BF16), wq_ref[...])
    for hp in range(2 * PEER_HEADS):
        qhp = q[:, hp * PEER_HALF:(hp + 1) * PEER_HALF].astype(BF16)
        st = _dot_nt(keys_ref[hp % 2], qhp)
        for lt in range(tm // LANES):
            st_ref[hp, lt] = st[:, lt * LANES:(lt + 1) * LANES]


def _peer_query(tok, x_all, sc, sh, wq, keys, n_tok, tm):
    d = D_MODEL
    row = tok.mod_row(tm)
    return pl.pallas_call(
        functools.partial(_peer_query_kernel, tm=tm),
        grid=(n_tok // tm,),
        in_specs=[
            pl.BlockSpec((tm, d), lambda i: (i, 0)),
            pl.BlockSpec((1, 1, d), lambda i: (row(i), 0, 0)),
            pl.BlockSpec((1, 1, d), lambda i: (row(i), 0, 0)),
            pl.BlockSpec((d, PEER_HEADS * PEER_QDIM), lambda i: (0, 0)),
            pl.BlockSpec((2, N_KEYS, PEER_HALF), lambda i: (0, 0, 0)),
        ],
        out_specs=[
            pl.BlockSpec((d, tm), lambda i: (0, i)),
            pl.BlockSpec((2 * PEER_HEADS, tm // LANES, N_KEYS, LANES), lambda i: (0, i, 0, 0)),
        ],
        out_shape=[
            jax.ShapeDtypeStruct((d, n_tok), BF16),
            jax.ShapeDtypeStruct((2 * PEER_HEADS, n_tok // LANES, N_KEYS, LANES), F32),
        ],
        compiler_params=_params("parallel"),
        name="peer_query",
    )(x_all, sc, sh, wq, keys)


def _peer_topk_kernel(st_ref, n_ref, c_ref, rank_ref, r_ref, top_ref, *, n_tiles):
    neg = -jnp.inf

    def tile(idx, carry):
        h = idx // n_tiles
        lt = idx % n_tiles
        s1 = st_ref[2 * h, lt]
        s2 = st_ref[2 * h + 1, lt]
        cur = s1
        for k in range(PEER_TOPK):
            m = jnp.max(cur, axis=0, keepdims=True)
            top_ref[0, k:k + 1, :] = m
            cur = jnp.where(cur >= m, neg, cur)
        cur = s2
        rank = jnp.full(s2.shape, float(N_KEYS), F32)
        for k in range(PEER_TOPK):
            m = jnp.max(cur, axis=0, keepdims=True)
            top_ref[1, k:k + 1, :] = m
            hit = cur >= m
            rank = jnp.where(hit, float(k), rank)
            cur = jnp.where(hit, neg, cur)
        v1 = top_ref[0]
        v2 = top_ref[1]
        pieces = [v1[0:1] + v2]
        for a in range(1, 8):
            pieces.append(v1[a:a + 1] + v2[0:8])
        pieces.append(v1[8:16] + v2[0:1])
        cand = jnp.concatenate(pieces, axis=0)
        m0 = jnp.max(cand, axis=0, keepdims=True)
        cand = jnp.where(cand >= m0, neg, cand)
        zsum = jnp.ones_like(m0)
        m = m0
        for k in range(1, PEER_TOPK):
            m = jnp.max(cand, axis=0, keepdims=True)
            zsum = zsum + jnp.exp(m - m0)
            cand = jnp.where(cand >= m, neg, cand)
        thr = m
        cnt_top = jnp.zeros(v1.shape, F32)
        for k in range(PEER_TOPK):
            cnt_top = cnt_top + jnp.where(v1 + v2[k:k + 1] >= thr, 1.0, 0.0)
        cnt = jnp.zeros(s1.shape, F32)
        for a in range(PEER_TOPK):
            cnt = jnp.where(s1 == v1[a:a + 1], cnt_top[a:a + 1], cnt)
        lanes = pl.ds(pl.multiple_of(lt * LANES, LANES), LANES)
        n_ref[h, :, lanes] = cnt
        c_ref[h, :, lanes] = jnp.exp(s1 - v1[0:1])
        rank_ref[h, lt] = rank
        r_ref[h, lt] = jnp.exp(s2 - v2[0:1]) * (0.5 / zsum)
        return carry

    lax.fori_loop(0, PEER_HEADS * n_tiles, tile, 0)


def _peer_topk(st, tl):
    n_tiles_all = st.shape[1]
    n_tok = n_tiles_all * LANES
    n_tiles = tl // LANES
    row_spec = pl.BlockSpec((PEER_HEADS, N_KEYS, tl), lambda i: (0, 0, i))
    row_shape = jax.ShapeDtypeStruct((PEER_HEADS, N_KEYS, n_tok), F32)
    tile_spec = pl.BlockSpec((PEER_HEADS, n_tiles, N_KEYS, LANES), lambda i: (0, i, 0, 0))
    tile_shape = jax.ShapeDtypeStruct((PEER_HEADS, n_tiles_all, N_KEYS, LANES), F32)
    return pl.pallas_call(
        functools.partial(_peer_topk_kernel, n_tiles=n_tiles),
        grid=(n_tok // tl,),
        in_specs=[pl.BlockSpec((2 * PEER_HEADS, n_tiles, N_KEYS, LANES), lambda i: (0, i, 0, 0))],
        out_specs=[row_spec, row_spec, tile_spec, tile_spec],
        out_shape=[row_shape, row_shape, tile_shape, tile_shape],
        scratch_shapes=[pltpu.VMEM((2, PEER_TOPK, LANES), F32)],
        compiler_params=_params("parallel"),
        name="peer_topk",
    )(st)


def _peer_main_kernel(h2t_ref, u_ref, vt_ref, rank_ref, r_ref, n_ref, c_ref, x_ref, g2_ref, lng_ref,
                      lnb_ref, o_ref, acc_ref, w_ref, st_ref, *, n_slab, tb, alpha):
    k = pl.program_id(1)

    @pl.when(k == 0)
    def _():
        acc_ref[...] = jnp.zeros_like(acc_ref)

    st = _dot(u_ref[...], h2t_ref[...])
    for lt in range(tb // LANES):
        st_ref[lt] = st[:, lt * LANES:(lt + 1) * LANES]

    for sl in range(n_slab):
        for lt in range(tb // LANES):
            lanes = slice(lt * LANES, (lt + 1) * LANES)
            g = None
            for h in range(PEER_HEADS):
                picked = rank_ref[h, lt] < n_ref[h, sl:sl + 1, lanes]
                term = jnp.where(picked, c_ref[h, sl:sl + 1, lanes] * r_ref[h, lt], 0.0)
                g = term if g is None else g + term
            x = st_ref[lt, sl * N_KEYS:(sl + 1) * N_KEYS, :]
            t = jnp.tanh(x * (GELU_K1 + GELU_K2 * (x * x)))
            w_ref[sl * N_KEYS:(sl + 1) * N_KEYS, lanes] = ((x * g) * (1.0 + t)).astype(BF16)
    acc_ref[...] += _dot(vt_ref[...], w_ref[...])

    @pl.when(k == pl.num_programs(1) - 1)
    def _():
        z = alpha * x_ref[...] + g2_ref[0] * acc_ref[...].T
        o_ref[...] = _layer_norm(z) * lng_ref[...] + lnb_ref[...]


def _peer_main(tok, x_all, h2t, n_cnt, c_fac, rank, r_fac, g2, u_tab, vt_tab, ln_g, ln_b, n_tok, tb,
               n_slab, alpha):
    d = D_MODEL
    ec = n_slab * N_KEYS
    row = tok.mod_row(tb)
    tile_spec = pl.BlockSpec((PEER_HEADS, tb // LANES, N_KEYS, LANES), lambda t, k: (0, t, 0, 0))
    return pl.pallas_call(
        functools.partial(_peer_main_kernel, n_slab=n_slab, tb=tb, alpha=alpha),
        grid=(n_tok // tb, N_EXPERTS // ec),
        in_specs=[
            pl.BlockSpec((d, tb), lambda t, k: (0, t)),
            pl.BlockSpec((ec, d), lambda t, k: (k, 0)),
            pl.BlockSpec((d, ec), lambda t, k: (0, k)),
            tile_spec,
            tile_spec,
            pl.BlockSpec((PEER_HEADS, n_slab, tb), lambda t, k: (0, k, t)),
            pl.BlockSpec((PEER_HEADS, n_slab, tb), lambda t, k: (0, k, t)),
            pl.BlockSpec((tb, d), lambda t, k: (t, 0)),
            pl.BlockSpec((1, 1, d), lambda t, k: (row(t), 0, 0)),
            pl.BlockSpec((1, d), lambda t, k: (0, 0)),
            pl.BlockSpec((1, d), lambda t, k: (0, 0)),
        ],
        out_specs=pl.BlockSpec((tb, d), lambda t, k: (t, 0)),
        out_shape=jax.ShapeDtypeStruct((n_tok, d), F32),
        scratch_shapes=[pltpu.VMEM((d, tb), F32), pltpu.VMEM((ec, tb), BF16),
                        pltpu.VMEM((tb // LANES, ec, LANES), F32)],
        compiler_params=_params("parallel", "arbitrary"),
        name="peer_main",
    )(h2t, u_tab, vt_tab, rank, r_fac, n_cnt, c_fac, x_all, g2, ln_g, ln_b)


def _rope_tables(seq, tm):
    rows = seq // GRID_W
    row = jnp.repeat(jnp.arange(rows, dtype=F32), GRID_W)
    col = jnp.tile(jnp.arange(GRID_W, dtype=F32), rows)
    n_freq = HEAD_DIM // 4
    inv = ROPE_THETA ** (-jnp.arange(n_freq, dtype=F32) / n_freq)
    ang = jnp.concatenate([row[:, None] * inv, col[:, None] * inv], -1)
    cos, sin = jnp.cos(ang), jnp.sin(ang)
    cos_t = jnp.tile(jnp.concatenate([cos, cos], -1), (1, LANES // HEAD_DIM))
    sin_t = jnp.tile(jnp.concatenate([-sin, sin], -1), (1, LANES // HEAD_DIM))
    cos_t = jnp.concatenate([cos_t, jnp.ones((tm, LANES), F32)], 0)
    sin_t = jnp.concatenate([sin_t, jnp.zeros((tm, LANES), F32)], 0)
    return cos_t, sin_t


def _row(v):
    return v.reshape(1, -1)


def kernel(x, c, ctx, c_ctx, w_mod, b_mod, w_in, a_ln_g, a_ln_b, a_ws, a_bs, a_wp, q_norm_g, k_norm_g, b_wp, c_conv_w, c_conv_b, c_ln_g, c_ln_b, c_wp, w_out, ln1_g, ln1_b, peer_wq, peer_keys, peer_u, peer_v, ln2_g, ln2_b):
    batch, seq, d = x.shape
    ctx_len = ctx.shape[1]
    depth = w_mod.shape[0]
    assert d == D_MODEL and batch < MOD_ROWS
    tok = _Tokens(batch, seq, ctx_len)
    alpha = float((2 * depth) ** 0.25)

    tm = min(256, ctx_len)
    tq = min(256, ctx_len)
    tb = min(512, batch * ctx_len)
    n_slab = 8
    assert seq % tm == 0 and ctx_len % tm == 0 and seq % tb == 0 and (batch * ctx_len) % tb == 0

    x_all = jnp.concatenate([x.reshape(batch * seq, d), ctx.reshape(batch * ctx_len, d)], 0)
    c_rows = jnp.concatenate([c, c_ctx[None], jnp.zeros((MOD_ROWS - batch - 1, d), F32)], 0)
    mod = _modulation(c_rows, w_mod, b_mod)

    cos_t, sin_t = _rope_tables(seq, tm)
    head_perm = np.concatenate([np.arange(0, HEAD_DIM, 2), np.arange(1, HEAD_DIM, 2)])
    q_cols = 2 * A_WIDTH + (np.arange(N_HEADS)[:, None] * HEAD_DIM + head_perm[None]).reshape(-1)
    k_cols = 2 * A_WIDTH + Q_DIM + (np.arange(N_KV_HEADS)[:, None] * HEAD_DIM + head_perm[None]).reshape(-1)
    v_cols = 2 * A_WIDTH + Q_DIM + KV_DIM + np.arange(KV_DIM)
    qkv_cols = np.concatenate([q_cols, k_cols, v_cols])
    o_glu = 2 * A_WIDTH + Q_DIM + 2 * KV_DIM
    o_gate = o_glu + 2 * C_WIDTH
    lane_head = np.arange(LANES) // HEAD_DIM
    bd = jnp.asarray((lane_head[:, None] == lane_head[None]) / HEAD_DIM, BF16)

    for i in range(depth):
        last = i == depth - 1
        n_mix = tok.n_lat if last else tok.n_all

        def mod_chunk(j, i=i):
            return mod[i, :, j * d:(j + 1) * d].reshape(MOD_ROWS, 1, d)

        sh1, sc1, g1, sh2, sc2, g2 = (mod_chunk(j) for j in range(6))
        w_i = w_in[i]
        w_uv = w_i[:, :2 * A_WIDTH].astype(BF16)
        w_qkv = w_i[:, qkv_cols].astype(BF16)
        w_glu = w_i[:, o_glu:o_gate].astype(BF16)
        w_gate = w_i[:, o_gate:].astype(BF16)

        uv = _ln_mod_matmul(tok, x_all, n_mix, sc1, sh1, w_uv, tb, 2 * A_WIDTH)
        qkv = _ln_mod_matmul(tok, x_all, tok.n_all, sc1, sh1, w_qkv, tb, Q_DIM + 2 * KV_DIM)
        glu = _ln_mod_matmul(tok, x_all, n_mix, sc1, sh1, w_glu, tb, 2 * C_WIDTH)
        gates = _ln_mod_matmul(tok, x_all, n_mix, sc1, sh1, w_gate, tb, 3 * d)

        ha = _gmlp_branch(uv, gates, _row(a_ln_g[i]), _row(a_ln_b[i]), a_ws[i].astype(BF16),
                          a_bs[i].T, a_wp[i].astype(BF16), n_mix, tm)
        conv_w = jnp.concatenate([c_conv_w[i], jnp.zeros((1, C_WIDTH), F32)], 0)
        hc = _conf_branch(tok, glu, gates, conv_w, _row(c_conv_b[i]), _row(c_ln_g[i]),
                          _row(c_ln_b[i]), c_wp[i].astype(BF16), n_mix, tm)
        gq = jnp.tile(q_norm_g[i][head_perm], LANES // HEAD_DIM).reshape(1, LANES)
        gk = jnp.tile(k_norm_g[i][head_perm], LANES // HEAD_DIM).reshape(1, LANES)
        q_r, k_ext, v_ext = _qkv_prep(tok, qkv, cos_t, sin_t, bd, gq, gk, tm)
        att = _attention(tok, q_r, k_ext, v_ext, True, tq)
        if not last:
            att = jnp.concatenate([att, _attention(tok, q_r, k_ext, v_ext, False, tq)], 0)
        x_mix = _merge(tok, x_all, ha, att, hc, gates, g1, b_wp[i].astype(BF16),
                       w_out[i].astype(BF16), _row(ln1_g[i]), _row(ln1_b[i]), n_mix, tm, alpha)

        h2t, st = _peer_query(tok, x_mix, sc2, sh2, peer_wq[i].astype(BF16),
                              peer_keys[i].astype(BF16), n_mix, tm)
        n_cnt, c_fac, rank, r_fac = _peer_topk(st, tm)
        x_new = _peer_main(tok, x_mix, h2t, n_cnt, c_fac, rank, r_fac, g2, peer_u[i].astype(BF16),
                           peer_v[i].T.astype(BF16), _row(ln2_g[i]), _row(ln2_b[i]), n_mix, tb,
                           n_slab, alpha)
        if last:
            return x_new.reshape(batch, seq, d)
        x_all = x_new
```

```python
import functools

import jax
import jax.numpy as jnp
import numpy as np
from jax import lax
from jax.experimental import pallas as pl
from jax.experimental.pallas import tpu as pltpu

D_MODEL = 1024
GRID_W = 64
CHUNK = 128
A_WIDTH = 1024
A_GROUPS = 4
N_HEADS = 16
N_KV_HEADS = 4
HEAD_DIM = 64
Q_DIM = N_HEADS * HEAD_DIM
KV_DIM = N_KV_HEADS * HEAD_DIM
ROPE_THETA = 10000.0
C_WIDTH = 1024
CONV_W = 31
CONV_HALO = 16
PEER_HEADS = 8
N_KEYS = 128
N_EXPERTS = N_KEYS * N_KEYS
PEER_TOPK = 16
PEER_QDIM = 256
PEER_HALF = PEER_QDIM // 2
EPS = 1e-6
MOD_ROWS = 16
LANES = 128
SUBLANES = 8
BF16_ROWS = 16
LOG2_E = float(np.log2(np.e))
ONES_LANE_LOW = 0
ONES_LANE_HIGH = HEAD_DIM
GELU_K1 = float(np.sqrt(2.0 / np.pi))
GELU_K2 = float(np.sqrt(2.0 / np.pi) * 0.044715)

BF16 = jnp.bfloat16
F32 = jnp.float32

VMEM_LIMIT_BYTES = 56 * 1024 * 1024


def _params(*sem):
    return pltpu.CompilerParams(dimension_semantics=sem, vmem_limit_bytes=VMEM_LIMIT_BYTES)


def _layer_norm(x):
    mu = jnp.mean(x, -1, keepdims=True)
    xc = x - mu
    var = jnp.mean(xc * xc, -1, keepdims=True)
    return xc * lax.rsqrt(var + EPS)


def _split_bf16(a):
    hi = a.astype(BF16)
    lo = (a - hi.astype(F32)).astype(BF16)
    return hi, lo


def _dot(a, b):
    return jnp.dot(a, b, preferred_element_type=F32)


def _dot_nt(a, b):
    return lax.dot_general(a, b, (((1,), (1,)), ((), ())), preferred_element_type=F32)


def _mod_kernel(c_ref, w_ref, b_ref, o_ref):
    c = c_ref[...]
    a_hi, a_lo = _split_bf16(c * jax.nn.sigmoid(c))
    w_hi, w_lo = _split_bf16(w_ref[0])
    o_ref[0] = _dot(a_hi, w_hi) + _dot(a_hi, w_lo) + _dot(a_lo, w_hi) + b_ref[0]


def _modulation(c_rows, w_mod, b_mod):
    depth, d, n = w_mod.shape
    tn = 1536
    return pl.pallas_call(
        _mod_kernel,
        grid=(depth, n // tn),
        in_specs=[
            pl.BlockSpec((MOD_ROWS, d), lambda l, j: (0, 0)),
            pl.BlockSpec((1, d, tn), lambda l, j: (l, 0, j)),
            pl.BlockSpec((1, 1, tn), lambda l, j: (l, 0, j)),
        ],
        out_specs=pl.BlockSpec((1, MOD_ROWS, tn), lambda l, j: (l, 0, j)),
        out_shape=jax.ShapeDtypeStruct((depth, MOD_ROWS, n), F32),
        compiler_params=_params("parallel", "parallel"),
        name="modulation",
    )(c_rows, w_mod, b_mod.reshape(depth, 1, n))


def _ln_mod_mm_kernel(x_ref, sc_ref, sh_ref, w_ref, o_ref):
    h = _layer_norm(x_ref[...]) * (1.0 + sc_ref[0]) + sh_ref[0]
    o_ref[...] = _dot(h.astype(BF16), w_ref[...]).astype(o_ref.dtype)


class _Tokens:
    def __init__(self, batch, seq, ctx_len):
        self.batch, self.seq, self.ctx_len = batch, seq, ctx_len
        self.n_lat = batch * seq
        self.n_all = self.n_lat + batch * ctx_len

    def mod_row(self, tm):
        n_lat_blocks = self.n_lat // tm
        per_seq = self.seq // tm
        batch = self.batch
        return lambda i: jnp.where(i < n_lat_blocks, i // per_seq, batch)


def _ln_mod_matmul(tok, x_all, n_tok, sc, sh, w, tm, tn, out_dtype=F32):
    d, n = w.shape
    row = tok.mod_row(tm)
    return pl.pallas_call(
        _ln_mod_mm_kernel,
        grid=(n // tn, n_tok // tm),
        in_specs=[
            pl.BlockSpec((tm, d), lambda j, i: (i, 0)),
            pl.BlockSpec((1, 1, d), lambda j, i: (row(i), 0, 0)),
            pl.BlockSpec((1, 1, d), lambda j, i: (row(i), 0, 0)),
            pl.BlockSpec((d, tn), lambda j, i: (0, j)),
        ],
        out_specs=pl.BlockSpec((tm, tn), lambda j, i: (i, j)),
        out_shape=jax.ShapeDtypeStruct((n_tok, n), out_dtype),
        compiler_params=_params("parallel", "parallel"),
        name="ln_mod_matmul",
    )(x_all, sc, sh, w)


def _gmlp_kernel(uv_ref, gate_ref, lng_ref, lnb_ref, ws_ref, bs_ref, wp_ref, o_ref, ha_ref, *, tm):
    u = jax.nn.gelu(uv_ref[:, :A_WIDTH])
    v = _layer_norm(jax.nn.gelu(uv_ref[:, A_WIDTH:])) * lng_ref[...] + lnb_ref[...]
    vb = v.astype(BF16)
    gw = A_WIDTH // A_GROUPS
    for ch in range(tm // CHUNK):
        r0 = ch * CHUNK
        for g in range(A_GROUPS):
            c0 = g * gw
            s = _dot(ws_ref[g], vb[r0:r0 + CHUNK, c0:c0 + gw])
            s = s + jnp.broadcast_to(bs_ref[:, g:g + 1], (CHUNK, gw))
            ha_ref[r0:r0 + CHUNK, c0:c0 + gw] = (u[r0:r0 + CHUNK, c0:c0 + gw] * s).astype(BF16)
    o_ref[...] = jax.nn.sigmoid(gate_ref[...]) * _dot(ha_ref[...], wp_ref[...])


def _gmlp_branch(uv, gates, ln_g, ln_b, ws, bs_t, wp, n_tok, tm):
    d = D_MODEL
    return pl.pallas_call(
        functools.partial(_gmlp_kernel, tm=tm),
        grid=(n_tok // tm,),
        in_specs=[
            pl.BlockSpec((tm, 2 * A_WIDTH), lambda i: (i, 0)),
            pl.BlockSpec((tm, d), lambda i: (i, 0)),
            pl.BlockSpec((1, A_WIDTH), lambda i: (0, 0)),
            pl.BlockSpec((1, A_WIDTH), lambda i: (0, 0)),
            pl.BlockSpec((A_GROUPS, CHUNK, CHUNK), lambda i: (0, 0, 0)),
            pl.BlockSpec((CHUNK, A_GROUPS), lambda i: (0, 0)),
            pl.BlockSpec((A_WIDTH, d), lambda i: (0, 0)),
        ],
        out_specs=pl.BlockSpec((tm, d), lambda i: (i, 0)),
        out_shape=jax.ShapeDtypeStruct((n_tok, d), F32),
        scratch_shapes=[pltpu.VMEM((tm, A_WIDTH), BF16)],
        compiler_params=_params("parallel"),
        name="gmlp_branch",
    )(uv, gates, ln_g, ln_b, ws, bs_t, wp)


def _conf_kernel(cur_ref, prev_ref, next_ref, gate_ref, w_ref, cb_ref, lng_ref, lnb_ref, wp_ref,
                 o_ref, z_ref, zs_ref, *, tm, n_lat_blocks, per_seq_lat, per_seq_ctx):
    i = pl.program_id(0)
    is_ctx = i >= n_lat_blocks
    pos = jnp.where(is_ctx, (i - n_lat_blocks) % per_seq_ctx, i % per_seq_lat)
    last = jnp.where(is_ctx, per_seq_ctx - 1, per_seq_lat - 1)

    def glu(ref):
        return ref[:, :C_WIDTH] * jax.nn.sigmoid(ref[:, C_WIDTH:])

    z_ref[0:CONV_HALO, :] = jnp.where(pos > 0, glu(prev_ref), 0.0)
    z_ref[CONV_HALO:CONV_HALO + tm, :] = glu(cur_ref)
    z_ref[CONV_HALO + tm:, :] = jnp.where(pos < last, glu(next_ref), 0.0)
    acc = jnp.zeros((tm, C_WIDTH), F32) + cb_ref[...]
    first = CONV_HALO - CONV_W // 2
    span = tm + (CONV_W // SUBLANES) * SUBLANES
    for r in range(SUBLANES):
        zs_ref[r] = z_ref[r:r + span, :]
    for r in range(SUBLANES):
        for off in range(first, first + CONV_W):
            if off % SUBLANES == r:
                k = off - first
                acc = acc + zs_ref[r, off - r:off - r + tm, :] * w_ref[k:k + 1, :]
    y = _layer_norm(acc) * lng_ref[...] + lnb_ref[...]
    y = y * jax.nn.sigmoid(y)
    o_ref[...] = jax.nn.sigmoid(gate_ref[...]) * _dot(y.astype(BF16), wp_ref[...])


def _conf_branch(tok, glu, gates, conv_w, conv_b, ln_g, ln_b, wp, n_tok, tm):
    d = D_MODEL
    hb = tm // CONV_HALO
    n_halo_blocks = n_tok // CONV_HALO
    kernel = functools.partial(_conf_kernel, tm=tm, n_lat_blocks=tok.n_lat // tm,
                               per_seq_lat=tok.seq // tm, per_seq_ctx=tok.ctx_len // tm)
    return pl.pallas_call(
        kernel,
        grid=(n_tok // tm,),
        in_specs=[
            pl.BlockSpec((tm, 2 * C_WIDTH), lambda i: (i, 0)),
            pl.BlockSpec((CONV_HALO, 2 * C_WIDTH), lambda i: (jnp.maximum(i * hb - 1, 0), 0)),
            pl.BlockSpec((CONV_HALO, 2 * C_WIDTH),
                         lambda i: (jnp.minimum((i + 1) * hb, n_halo_blocks - 1), 0)),
            pl.BlockSpec((tm, d), lambda i: (i, 2)),
            pl.BlockSpec((CONV_W + 1, C_WIDTH), lambda i: (0, 0)),
            pl.BlockSpec((1, C_WIDTH), lambda i: (0, 0)),
            pl.BlockSpec((1, C_WIDTH), lambda i: (0, 0)),
            pl.BlockSpec((1, C_WIDTH), lambda i: (0, 0)),
            pl.BlockSpec((C_WIDTH, d), lambda i: (0, 0)),
        ],
        out_specs=pl.BlockSpec((tm, d), lambda i: (i, 0)),
        out_shape=jax.ShapeDtypeStruct((n_tok, d), F32),
        scratch_shapes=[pltpu.VMEM((tm + 2 * CONV_HALO, C_WIDTH), F32),
                        pltpu.VMEM((SUBLANES, tm + (CONV_W // SUBLANES) * SUBLANES, C_WIDTH), F32)],
        compiler_params=_params("parallel"),
        name="conformer_branch",
    )(glu, glu, glu, gates, conv_w, conv_b, ln_g, ln_b, wp)


def _qkv_prep_kernel(qkv_ref, cos_ref, sin_ref, bd_ref, gq_ref, gk_ref, q_ref, k_ref, v_ref, *, tm):
    lane = lax.broadcasted_iota(jnp.int32, (tm, LANES), 1)
    first_half = (lane % HEAD_DIM) < (HEAD_DIM // 2)
    low_head = lane < HEAD_DIM
    cos = cos_ref[...]
    sin = sin_ref[...]
    bd = bd_ref[...]

    def norm_rope(x, g):
        ms = _dot((x * x).astype(BF16), bd)
        xn = x * lax.rsqrt(ms + EPS) * g
        swapped = jnp.where(first_half, pltpu.roll(xn, LANES - HEAD_DIM // 2, 1),
                            pltpu.roll(xn, HEAD_DIM // 2, 1))
        return xn * cos + swapped * sin

    def head_masked(y, base, ref, fill_low, fill_high):
        rolled = pltpu.roll(y, HEAD_DIM, 1)
        ref[:, base:base + LANES] = jnp.where(low_head, y, fill_high).astype(BF16)
        ref[:, base + LANES:base + 2 * LANES] = jnp.where(low_head, fill_low, rolled).astype(BF16)
        ref[:, base + 2 * LANES:base + 3 * LANES] = jnp.where(low_head, rolled, fill_high).astype(BF16)
        ref[:, base + 3 * LANES:base + 4 * LANES] = jnp.where(low_head, fill_low, y).astype(BF16)

    zero = jnp.zeros((tm, LANES), F32)
    ones_low = jnp.where(lane == ONES_LANE_LOW, 1.0, 0.0)
    ones_high = jnp.where(lane == ONES_LANE_HIGH, 1.0, 0.0)
    scale = HEAD_DIM ** -0.5 * LOG2_E
    for c in range(Q_DIM // LANES):
        y = norm_rope(qkv_ref[:, c * LANES:(c + 1) * LANES], gq_ref[...])
        q_ref[:, c * LANES:(c + 1) * LANES] = (y * scale).astype(BF16)
    for c in range(KV_DIM // LANES):
        y = norm_rope(qkv_ref[:, Q_DIM + c * LANES:Q_DIM + (c + 1) * LANES], gk_ref[...])
        head_masked(y, c * 4 * LANES, k_ref, zero, zero)
        head_masked(qkv_ref[:, Q_DIM + KV_DIM + c * LANES:Q_DIM + KV_DIM + (c + 1) * LANES],
                    c * 4 * LANES, v_ref, ones_low, ones_high)


def _qkv_prep(tok, qkv, cos_t, sin_t, bd, gq, gk, tm):
    n_tok = qkv.shape[0]
    n_lat_blocks = tok.n_lat // tm
    per_seq = tok.seq // tm
    ext = N_KV_HEADS * 2 * LANES

    def tab(i):
        return (jnp.where(i < n_lat_blocks, i % per_seq, per_seq), 0)

    return pl.pallas_call(
        functools.partial(_qkv_prep_kernel, tm=tm),
        grid=(n_tok // tm,),
        in_specs=[
            pl.BlockSpec((tm, Q_DIM + 2 * KV_DIM), lambda i: (i, 0)),
            pl.BlockSpec((tm, LANES), tab),
            pl.BlockSpec((tm, LANES), tab),
            pl.BlockSpec((LANES, LANES), lambda i: (0, 0)),
            pl.BlockSpec((1, LANES), lambda i: (0, 0)),
            pl.BlockSpec((1, LANES), lambda i: (0, 0)),
        ],
        out_specs=[
            pl.BlockSpec((tm, Q_DIM), lambda i: (i, 0)),
            pl.BlockSpec((tm, ext), lambda i: (i, 0)),
            pl.BlockSpec((tm, ext), lambda i: (i, 0)),
        ],
        out_shape=[
            jax.ShapeDtypeStruct((n_tok, Q_DIM), BF16),
            jax.ShapeDtypeStruct((n_tok, ext), BF16),
            jax.ShapeDtypeStruct((n_tok, ext), BF16),
        ],
        compiler_params=_params("parallel"),
        name="qkv_prep",
    )(qkv, cos_t, sin_t, bd, gq, gk)


def _attn_kernel(q_ref, kc_ref, vc_ref, *rest, with_latent):
    if with_latent:
        kl_ref, vl_ref, o_ref = rest
    else:
        (o_ref,) = rest
    low_head = lax.broadcasted_iota(jnp.int32, (q_ref.shape[0], LANES), 1) < HEAD_DIM
    for pair in range(N_HEADS // 2):
        qc = q_ref[:, pair * LANES:(pair + 1) * LANES]
        outs = []
        for par in range(2):
            head = 2 * pair + par
            col = (head // (N_HEADS // N_KV_HEADS)) * 2 * LANES + par * LANES
            s_c = _dot_nt(qc, kc_ref[:, col:col + LANES])
            m = jnp.max(s_c, -1, keepdims=True)
            if with_latent:
                s_l = _dot_nt(qc, kl_ref[:, col:col + LANES])
                m = jnp.maximum(m, jnp.max(s_l, -1, keepdims=True))
            o = _dot(jnp.exp2((s_c - m).astype(BF16)), vc_ref[:, col:col + LANES])
            if with_latent:
                o = o + _dot(jnp.exp2((s_l - m).astype(BF16)), vl_ref[:, col:col + LANES])
            ones_lane = ONES_LANE_HIGH if par == 0 else ONES_LANE_LOW
            outs.append(o / o[:, ones_lane:ones_lane + 1])
        o_ref[:, pair * LANES:(pair + 1) * LANES] = jnp.where(low_head, outs[0], outs[1]).astype(o_ref.dtype)


def _attention(tok, q, k_ext, v_ext, latent, tq):
    batch, seq, ctx_len = tok.batch, tok.seq, tok.ctx_len
    ext = k_ext.shape[1]
    ctx_block0 = tok.n_lat // ctx_len
    if latent:
        per = seq // tq
        grid = (batch, per)
        q_map = lambda b, i: (b * per + i, 0)
        n_out = tok.n_lat
        out_map = q_map
    else:
        per = ctx_len // tq
        q0 = tok.n_lat // tq
        grid = (batch, per)
        q_map = lambda b, i: (q0 + b * per + i, 0)
        n_out = batch * ctx_len
        out_map = lambda b, i: (b * per + i, 0)
    in_specs = [
        pl.BlockSpec((tq, Q_DIM), q_map),
        pl.BlockSpec((ctx_len, ext), lambda b, i: (ctx_block0 + b, 0)),
        pl.BlockSpec((ctx_len, ext), lambda b, i: (ctx_block0 + b, 0)),
    ]
    args = [q, k_ext, v_ext]
    if latent:
        in_specs += [pl.BlockSpec((seq, ext), lambda b, i: (b, 0)),
                     pl.BlockSpec((seq, ext), lambda b, i: (b, 0))]
        args += [k_ext, v_ext]
    return pl.pallas_call(
        functools.partial(_attn_kernel, with_latent=latent),
        grid=grid,
        in_specs=in_specs,
        out_specs=pl.BlockSpec((tq, Q_DIM), out_map),
        out_shape=jax.ShapeDtypeStruct((n_out, Q_DIM), BF16),
        compiler_params=_params("parallel", "parallel"),
        name="attention_latent" if latent else "attention_context",
    )(*args)


def _merge_kernel(x_ref, a_ref, att_ref, c_ref, gate_ref, g1_ref, bwp_ref, wout_ref, lng_ref,
                  lnb_ref, o_ref, *, alpha):
    m = a_ref[...] + c_ref[...] + jax.nn.sigmoid(gate_ref[...]) * _dot(att_ref[...], bwp_ref[...])
    y = _dot(m.astype(BF16), wout_ref[...])
    z = alpha * x_ref[...] + g1_ref[0] * y
    o_ref[...] = _layer_norm(z) * lng_ref[...] + lnb_ref[...]


def _merge(tok, x_all, ha, att, hc, gates, g1, b_wp, w_out, ln_g, ln_b, n_tok, tm, alpha):
    d = D_MODEL
    row = tok.mod_row(tm)
    tok_spec = pl.BlockSpec((tm, d), lambda i: (i, 0))
    vec_spec = pl.BlockSpec((1, d), lambda i: (0, 0))
    mat_spec = pl.BlockSpec((d, d), lambda i: (0, 0))
    return pl.pallas_call(
        functools.partial(_merge_kernel, alpha=alpha),
        grid=(n_tok // tm,),
        in_specs=[tok_spec, tok_spec, tok_spec, tok_spec,
                  pl.BlockSpec((tm, d), lambda i: (i, 1)),
                  pl.BlockSpec((1, 1, d), lambda i: (row(i), 0, 0)),
                  mat_spec, mat_spec, vec_spec, vec_spec],
        out_specs=tok_spec,
        out_shape=jax.ShapeDtypeStruct((n_tok, d), F32),
        compiler_params=_params("parallel"),
        name="merge",
    )(x_all, ha, att, hc, gates, g1, b_wp, w_out, ln_g, ln_b)


def _peer_query_kernel(x_ref, sc_ref, sh_ref, wq_ref, keys_ref, h2t_ref, st_ref, *, tm):
    h2 = _layer_norm(x_ref[...]) * (1.0 + sc_ref[0]) + sh_ref[0]
    h2t_ref[...] = h2.T.astype(BF16)
    q = _dot(h2.astype(BF16), wq_ref[...])
    for hp in range(2 * PEER_HEADS):
        qhp = q[:, hp * PEER_HALF:(hp + 1) * PEER_HALF].astype(BF16)
        st = _dot_nt(keys_ref[hp % 2], qhp)
        for lt in range(tm // LANES):
            st_ref[hp, lt] = st[:, lt * LANES:(lt + 1) * LANES]


def _peer_query(tok, x_all, sc, sh, wq, keys, n_tok, tm):
    d = D_MODEL
    row = tok.mod_row(tm)
    return pl.pallas_call(
        functools.partial(_peer_query_kernel, tm=tm),
        grid=(n_tok // tm,),
        in_specs=[
            pl.BlockSpec((tm, d), lambda i: (i, 0)),
            pl.BlockSpec((1, 1, d), lambda i: (row(i), 0, 0)),
            pl.BlockSpec((1, 1, d), lambda i: (row(i), 0, 0)),
            pl.BlockSpec((d, PEER_HEADS * PEER_QDIM), lambda i: (0, 0)),
            pl.BlockSpec((2, N_KEYS, PEER_HALF), lambda i: (0, 0, 0)),
        ],
        out_specs=[
            pl.BlockSpec((d, tm), lambda i: (0, i)),
            pl.BlockSpec((2 * PEER_HEADS, tm // LANES, N_KEYS, LANES), lambda i: (0, i, 0, 0)),
        ],
        out_shape=[
            jax.ShapeDtypeStruct((d, n_tok), BF16),
            jax.ShapeDtypeStruct((2 * PEER_HEADS, n_tok // LANES, N_KEYS, LANES), F32),
        ],
        compiler_params=_params("parallel"),
        name="peer_query",
    )(x_all, sc, sh, wq, keys)


def _peer_topk_kernel(st_ref, n_ref, c_ref, rank_ref, r_ref, top_ref, *, n_tiles):
    neg = -jnp.inf

    def tile(idx, carry):
        h = idx // n_tiles
        lt = idx % n_tiles
        s1 = st_ref[2 * h, lt]
        s2 = st_ref[2 * h + 1, lt]
        cur = s1
        for k in range(PEER_TOPK):
            m = jnp.max(cur, axis=0, keepdims=True)
            top_ref[0, k:k + 1, :] = m
            cur = jnp.where(cur >= m, neg, cur)
        cur = s2
        rank = jnp.full(s2.shape, float(N_KEYS), F32)
        for k in range(PEER_TOPK):
            m = jnp.max(cur, axis=0, keepdims=True)
            top_ref[1, k:k + 1, :] = m
            hit = cur >= m
            rank = jnp.where(hit, float(k), rank)
            cur = jnp.where(hit, neg, cur)
        v1 = top_ref[0]
        v2 = top_ref[1]
        pieces = [v1[0:1] + v2]
        for a in range(1, 8):
            pieces.append(v1[a:a + 1] + v2[0:8])
        pieces.append(v1[8:16] + v2[0:1])
        cand = jnp.concatenate(pieces, axis=0)
        m0 = jnp.max(cand, axis=0, keepdims=True)
        cand = jnp.where(cand >= m0, neg, cand)
        zsum = jnp.ones_like(m0)
        m = m0
        for k in range(1, PEER_TOPK):
            m = jnp.max(cand, axis=0, keepdims=True)
            zsum = zsum + jnp.exp(m - m0)
            cand = jnp.where(cand >= m, neg, cand)
        thr = m
        cnt_top = jnp.zeros(v1.shape, F32)
        for k in range(PEER_TOPK):
            cnt_top = cnt_top + jnp.where(v1 + v2[k:k + 1] >= thr, 1.0, 0.0)
        cnt = jnp.zeros(s1.shape, F32)
        for a in range(PEER_TOPK):
            cnt = jnp.where(s1 == v1[a:a + 1], cnt_top[a:a + 1], cnt)
        lanes = pl.ds(pl.multiple_of(lt * LANES, LANES), LANES)
        n_ref[h, :, lanes] = cnt
        c_ref[h, :, lanes] = jnp.exp(s1 - v1[0:1])
        rank_ref[h, lt] = rank.astype(BF16)
        r_ref[h, lt] = (jnp.exp(s2 - v2[0:1]) * (0.5 / zsum)).astype(BF16)
        return carry

    lax.fori_loop(0, PEER_HEADS * n_tiles, tile, 0)


def _peer_topk(st, tl):
    n_tiles_all = st.shape[1]
    n_tok = n_tiles_all * LANES
    n_tiles = tl // LANES
    row_spec = pl.BlockSpec((PEER_HEADS, N_KEYS, tl), lambda i: (0, 0, i))
    row_shape = jax.ShapeDtypeStruct((PEER_HEADS, N_KEYS, n_tok), F32)
    tile_spec = pl.BlockSpec((PEER_HEADS, n_tiles, N_KEYS, LANES), lambda i: (0, i, 0, 0))
    tile_shape = jax.ShapeDtypeStruct((PEER_HEADS, n_tiles_all, N_KEYS, LANES), BF16)
    return pl.pallas_call(
        functools.partial(_peer_topk_kernel, n_tiles=n_tiles),
        grid=(n_tok // tl,),
        in_specs=[pl.BlockSpec((2 * PEER_HEADS, n_tiles, N_KEYS, LANES), lambda i: (0, i, 0, 0))],
        out_specs=[row_spec, row_spec, tile_spec, tile_spec],
        out_shape=[row_shape, row_shape, tile_shape, tile_shape],
        scratch_shapes=[pltpu.VMEM((2, PEER_TOPK, LANES), F32)],
        compiler_params=_params("parallel"),
        name="peer_topk",
    )(st)


def _peer_main_kernel(h2t_ref, u_ref, vt_ref, rank_ref, r_ref, n_ref, c_ref, x_ref, g2_ref, lng_ref,
                      lnb_ref, o_ref, acc_ref, w_ref, st_ref, *, n_slab, tb, alpha):
    k = pl.program_id(1)

    @pl.when(k == 0)
    def _():
        acc_ref[...] = jnp.zeros_like(acc_ref)

    st = _dot(u_ref[...], h2t_ref[...])
    for lt in range(tb // LANES):
        st_ref[lt] = st[:, lt * LANES:(lt + 1) * LANES]

    def row_bf16(ref, h, sl, lanes):
        return jnp.broadcast_to(ref[h, sl:sl + 1, lanes], (BF16_ROWS, LANES)).astype(BF16)

    groups = [slice(r0, r0 + BF16_ROWS) for r0 in range(0, N_KEYS, BF16_ROWS)]
    for sl in range(n_slab):
        for lt in range(tb // LANES):
            lanes = slice(lt * LANES, (lt + 1) * LANES)
            g = [None] * len(groups)
            for h in range(PEER_HEADS):
                cnt = row_bf16(n_ref, h, sl, lanes)
                cfac = row_bf16(c_ref, h, sl, lanes)
                for gi, rows in enumerate(groups):
                    picked_c = jnp.minimum(jnp.maximum(cnt - rank_ref[h, lt, rows, :], 0.0), cfac)
                    term = picked_c * r_ref[h, lt, rows, :]
                    g[gi] = term if g[gi] is None else g[gi] + term
            for gi, rows in enumerate(groups):
                e0 = sl * N_KEYS + rows.start
                x = st_ref[lt, e0:e0 + BF16_ROWS, :].astype(BF16)
                t = jnp.tanh(x * (GELU_K1 + GELU_K2 * (x * x)))
                w_ref[e0:e0 + BF16_ROWS, lanes] = (x * (1.0 + t)) * g[gi]
    acc_ref[...] += _dot(vt_ref[...], w_ref[...])

    @pl.when(k == pl.num_programs(1) - 1)
    def _():
        z = alpha * x_ref[...] + g2_ref[0] * acc_ref[...].T
        o_ref[...] = _layer_norm(z) * lng_ref[...] + lnb_ref[...]


def _peer_main(tok, x_all, h2t, n_cnt, c_fac, rank, r_fac, g2, u_tab, vt_tab, ln_g, ln_b, n_tok, tb,
               n_slab, alpha):
    d = D_MODEL
    ec = n_slab * N_KEYS
    row = tok.mod_row(tb)
    tile_spec = pl.BlockSpec((PEER_HEADS, tb // LANES, N_KEYS, LANES), lambda t, k: (0, t, 0, 0))
    return pl.pallas_call(
        functools.partial(_peer_main_kernel, n_slab=n_slab, tb=tb, alpha=alpha),
        grid=(n_tok // tb, N_EXPERTS // ec),
        in_specs=[
            pl.BlockSpec((d, tb), lambda t, k: (0, t)),
            pl.BlockSpec((ec, d), lambda t, k: (k, 0)),
            pl.BlockSpec((d, ec), lambda t, k: (0, k)),
            tile_spec,
            tile_spec,
            pl.BlockSpec((PEER_HEADS, n_slab, tb), lambda t, k: (0, k, t)),
            pl.BlockSpec((PEER_HEADS, n_slab, tb), lambda t, k: (0, k, t)),
            pl.BlockSpec((tb, d), lambda t, k: (t, 0)),
            pl.BlockSpec((1, 1, d), lambda t, k: (row(t), 0, 0)),
            pl.BlockSpec((1, d), lambda t, k: (0, 0)),
            pl.BlockSpec((1, d), lambda t, k: (0, 0)),
        ],
        out_specs=pl.BlockSpec((tb, d), lambda t, k: (t, 0)),
        out_shape=jax.ShapeDtypeStruct((n_tok, d), F32),
        scratch_shapes=[pltpu.VMEM((d, tb), F32), pltpu.VMEM((ec, tb), BF16),
                        pltpu.VMEM((tb // LANES, ec, LANES), F32)],
        compiler_params=_params("parallel", "arbitrary"),
        name="peer_main",
    )(h2t, u_tab, vt_tab, rank, r_fac, n_cnt, c_fac, x_all, g2, ln_g, ln_b)


def _rope_tables(seq, tm):
    rows = seq // GRID_W
    row = jnp.repeat(jnp.arange(rows, dtype=F32), GRID_W)
    col = jnp.tile(jnp.arange(GRID_W, dtype=F32), rows)
    n_freq = HEAD_DIM // 4
    inv = ROPE_THETA ** (-jnp.arange(n_freq, dtype=F32) / n_freq)
    ang = jnp.concatenate([row[:, None] * inv, col[:, None] * inv], -1)
    cos, sin = jnp.cos(ang), jnp.sin(ang)
    cos_t = jnp.tile(jnp.concatenate([cos, cos], -1), (1, LANES // HEAD_DIM))
    sin_t = jnp.tile(jnp.concatenate([-sin, sin], -1), (1, LANES // HEAD_DIM))
    cos_t = jnp.concatenate([cos_t, jnp.ones((tm, LANES), F32)], 0)
    sin_t = jnp.concatenate([sin_t, jnp.zeros((tm, LANES), F32)], 0)
    return cos_t, sin_t


def _row(v):
    return v.reshape(1, -1)


def kernel(x, c, ctx, c_ctx, w_mod, b_mod, w_in, a_ln_g, a_ln_b, a_ws, a_bs, a_wp, q_norm_g, k_norm_g, b_wp, c_conv_w, c_conv_b, c_ln_g, c_ln_b, c_wp, w_out, ln1_g, ln1_b, peer_wq, peer_keys, peer_u, peer_v, ln2_g, ln2_b):
    batch, seq, d = x.shape
    ctx_len = ctx.shape[1]
    depth = w_mod.shape[0]
    assert d == D_MODEL and batch < MOD_ROWS
    tok = _Tokens(batch, seq, ctx_len)
    alpha = float((2 * depth) ** 0.25)

    tm = min(256, ctx_len)
    tq = min(256, ctx_len)
    tb = min(512, batch * ctx_len)
    n_slab = 16
    assert seq % tm == 0 and ctx_len % tm == 0 and seq % tb == 0 and (batch * ctx_len) % tb == 0

    x_all = jnp.concatenate([x.reshape(batch * seq, d), ctx.reshape(batch * ctx_len, d)], 0)
    c_rows = jnp.concatenate([c, c_ctx[None], jnp.zeros((MOD_ROWS - batch - 1, d), F32)], 0)
    mod = _modulation(c_rows, w_mod, b_mod)

    cos_t, sin_t = _rope_tables(seq, tm)
    head_perm = np.concatenate([np.arange(0, HEAD_DIM, 2), np.arange(1, HEAD_DIM, 2)])
    q_cols = 2 * A_WIDTH + (np.arange(N_HEADS)[:, None] * HEAD_DIM + head_perm[None]).reshape(-1)
    k_cols = 2 * A_WIDTH + Q_DIM + (np.arange(N_KV_HEADS)[:, None] * HEAD_DIM + head_perm[None]).reshape(-1)
    v_cols = 2 * A_WIDTH + Q_DIM + KV_DIM + np.arange(KV_DIM)
    qkv_cols = np.concatenate([q_cols, k_cols, v_cols])
    o_glu = 2 * A_WIDTH + Q_DIM + 2 * KV_DIM
    o_gate = o_glu + 2 * C_WIDTH
    lane_head = np.arange(LANES) // HEAD_DIM
    bd = jnp.asarray((lane_head[:, None] == lane_head[None]) / HEAD_DIM, BF16)

    for i in range(depth):
        last = i == depth - 1
        n_mix = tok.n_lat if last else tok.n_all

        def mod_chunk(j, i=i):
            return mod[i, :, j * d:(j + 1) * d].reshape(MOD_ROWS, 1, d)

        sh1, sc1, g1, sh2, sc2, g2 = (mod_chunk(j) for j in range(6))
        w_i = w_in[i]
        w_uv = w_i[:, :2 * A_WIDTH].astype(BF16)
        w_qkv = w_i[:, qkv_cols].astype(BF16)
        w_glu = w_i[:, o_glu:o_gate].astype(BF16)
        w_gate = w_i[:, o_gate:].astype(BF16)

        uv = _ln_mod_matmul(tok, x_all, n_mix, sc1, sh1, w_uv, tb, 2 * A_WIDTH)
        qkv = _ln_mod_matmul(tok, x_all, tok.n_all, sc1, sh1, w_qkv, tb, Q_DIM + 2 * KV_DIM)
        glu = _ln_mod_matmul(tok, x_all, n_mix, sc1, sh1, w_glu, tb, 2 * C_WIDTH)
        gates = _ln_mod_matmul(tok, x_all, n_mix, sc1, sh1, w_gate, tb, 3 * d)

        ha = _gmlp_branch(uv, gates, _row(a_ln_g[i]), _row(a_ln_b[i]), a_ws[i].astype(BF16),
                          a_bs[i].T, a_wp[i].astype(BF16), n_mix, tm)
        conv_w = jnp.concatenate([c_conv_w[i], jnp.zeros((1, C_WIDTH), F32)], 0)
        hc = _conf_branch(tok, glu, gates, conv_w, _row(c_conv_b[i]), _row(c_ln_g[i]),
                          _row(c_ln_b[i]), c_wp[i].astype(BF16), n_mix, tm)
        gq = jnp.tile(q_norm_g[i][head_perm], LANES // HEAD_DIM).reshape(1, LANES)
        gk = jnp.tile(k_norm_g[i][head_perm], LANES // HEAD_DIM).reshape(1, LANES)
        q_r, k_ext, v_ext = _qkv_prep(tok, qkv, cos_t, sin_t, bd, gq, gk, tm)
        att = _attention(tok, q_r, k_ext, v_ext, True, tq)
        if not last:
            att = jnp.concatenate([att, _attention(tok, q_r, k_ext, v_ext, False, tq)], 0)
        x_mix = _merge(tok, x_all, ha, att, hc, gates, g1, b_wp[i].astype(BF16),
                       w_out[i].astype(BF16), _row(ln1_g[i]), _row(ln1_b[i]), n_mix, tm, alpha)

        h2t, st = _peer_query(tok, x_mix, sc2, sh2, peer_wq[i].astype(BF16),
                              peer_keys[i].astype(BF16), n_mix, tm)
        n_cnt, c_fac, rank, r_fac = _peer_topk(st, tm)
        x_new = _peer_main(tok, x_mix, h2t, n_cnt, c_fac, rank, r_fac, g2, peer_u[i].astype(BF16),
                           peer_v[i].T.astype(BF16), _row(ln2_g[i]), _row(ln2_b[i]), n_mix, tb,
                           n_slab, alpha)
        if last:
            return x_new.reshape(batch, seq, d)
        x_all = x_new
```

```python
import functools

import jax
import jax.numpy as jnp
import numpy as np
from jax import lax
from jax.experimental import pallas as pl
from jax.experimental.pallas import tpu as pltpu

D_MODEL = 1024
GRID_W = 64
CHUNK = 128
A_WIDTH = 1024
A_GROUPS = 4
N_HEADS = 16
N_KV_HEADS = 4
HEAD_DIM = 64
Q_DIM = N_HEADS * HEAD_DIM
KV_DIM = N_KV_HEADS * HEAD_DIM
ROPE_THETA = 10000.0
C_WIDTH = 1024
CONV_W = 31
CONV_HALO = 16
PEER_HEADS = 8
N_KEYS = 128
N_EXPERTS = N_KEYS * N_KEYS
PEER_TOPK = 16
PEER_QDIM = 256
PEER_HALF = PEER_QDIM // 2
EPS = 1e-6
MOD_ROWS = 16
LANES = 128
SUBLANES = 8
BF16_ROWS = 16
LOG2_E = float(np.log2(np.e))
MIX_UV = 0
MIX_GLU = 2 * A_WIDTH
MIX_GATES = 2 * A_WIDTH + 2 * C_WIDTH
ONES_LANE_LOW = 0
ONES_LANE_HIGH = HEAD_DIM
GELU_K1 = float(np.sqrt(2.0 / np.pi))
GELU_K2 = float(np.sqrt(2.0 / np.pi) * 0.044715)

BF16 = jnp.bfloat16
F32 = jnp.float32

VMEM_LIMIT_BYTES = 56 * 1024 * 1024


def _params(*sem):
    return pltpu.CompilerParams(dimension_semantics=sem, vmem_limit_bytes=VMEM_LIMIT_BYTES)


def _layer_norm(x):
    mu = jnp.mean(x, -1, keepdims=True)
    xc = x - mu
    var = jnp.mean(xc * xc, -1, keepdims=True)
    return xc * lax.rsqrt(var + EPS)


def _split_bf16(a):
    hi = a.astype(BF16)
    lo = (a - hi.astype(F32)).astype(BF16)
    return hi, lo


def _dot(a, b):
    return jnp.dot(a, b, preferred_element_type=F32)


def _dot_nt(a, b):
    return lax.dot_general(a, b, (((1,), (1,)), ((), ())), preferred_element_type=F32)


def _mod_kernel(c_ref, w_ref, b_ref, o_ref):
    c = c_ref[...]
    a_hi, a_lo = _split_bf16(c * jax.nn.sigmoid(c))
    w_hi, w_lo = _split_bf16(w_ref[0])
    o_ref[0] = _dot(a_hi, w_hi) + _dot(a_hi, w_lo) + _dot(a_lo, w_hi) + b_ref[0]


def _modulation(c_rows, w_mod, b_mod):
    depth, d, n = w_mod.shape
    tn = 1536
    return pl.pallas_call(
        _mod_kernel,
        grid=(depth, n // tn),
        in_specs=[
            pl.BlockSpec((MOD_ROWS, d), lambda l, j: (0, 0)),
            pl.BlockSpec((1, d, tn), lambda l, j: (l, 0, j)),
            pl.BlockSpec((1, 1, tn), lambda l, j: (l, 0, j)),
        ],
        out_specs=pl.BlockSpec((1, MOD_ROWS, tn), lambda l, j: (l, 0, j)),
        out_shape=jax.ShapeDtypeStruct((depth, MOD_ROWS, n), F32),
        compiler_params=_params("parallel", "parallel"),
        name="modulation",
    )(c_rows, w_mod, b_mod.reshape(depth, 1, n))


def _ln_mod_mm_kernel(x_ref, sc_ref, sh_ref, w_ref, o_ref):
    h = _layer_norm(x_ref[...]) * (1.0 + sc_ref[0]) + sh_ref[0]
    o_ref[...] = _dot(h.astype(BF16), w_ref[...]).astype(o_ref.dtype)


class _Tokens:
    def __init__(self, batch, seq, ctx_len):
        self.batch, self.seq, self.ctx_len = batch, seq, ctx_len
        self.n_lat = batch * seq
        self.n_all = self.n_lat + batch * ctx_len

    def mod_row(self, tm):
        n_lat_blocks = self.n_lat // tm
        per_seq = self.seq // tm
        batch = self.batch
        return lambda i: jnp.where(i < n_lat_blocks, i // per_seq, batch)


def _ln_mod_matmul(tok, x_all, n_tok, sc, sh, w, tm, tn, out_dtype=F32):
    d, n = w.shape
    row = tok.mod_row(tm)
    return pl.pallas_call(
        _ln_mod_mm_kernel,
        grid=(n // tn, n_tok // tm),
        in_specs=[
            pl.BlockSpec((tm, d), lambda j, i: (i, 0)),
            pl.BlockSpec((1, 1, d), lambda j, i: (row(i), 0, 0)),
            pl.BlockSpec((1, 1, d), lambda j, i: (row(i), 0, 0)),
            pl.BlockSpec((d, tn), lambda j, i: (0, j)),
        ],
        out_specs=pl.BlockSpec((tm, tn), lambda j, i: (i, j)),
        out_shape=jax.ShapeDtypeStruct((n_tok, n), out_dtype),
        compiler_params=_params("parallel", "parallel"),
        name="ln_mod_matmul",
    )(x_all, sc, sh, w)


def _gmlp_kernel(uv_ref, gate_ref, lng_ref, lnb_ref, ws_ref, bs_ref, wp_ref, o_ref, ha_ref, *, tm):
    u = jax.nn.gelu(uv_ref[:, :A_WIDTH].astype(F32))
    v = _layer_norm(jax.nn.gelu(uv_ref[:, A_WIDTH:].astype(F32))) * lng_ref[...] + lnb_ref[...]
    vb = v.astype(BF16)
    gw = A_WIDTH // A_GROUPS
    for ch in range(tm // CHUNK):
        r0 = ch * CHUNK
        for g in range(A_GROUPS):
            c0 = g * gw
            s = _dot(ws_ref[g], vb[r0:r0 + CHUNK, c0:c0 + gw])
            s = s + jnp.broadcast_to(bs_ref[:, g:g + 1], (CHUNK, gw))
            ha_ref[r0:r0 + CHUNK, c0:c0 + gw] = (u[r0:r0 + CHUNK, c0:c0 + gw] * s).astype(BF16)
    o_ref[...] = jax.nn.sigmoid(gate_ref[...].astype(F32)) * _dot(ha_ref[...], wp_ref[...])


def _gmlp_branch(mix, ln_g, ln_b, ws, bs_t, wp, n_tok, tm):
    d = D_MODEL
    return pl.pallas_call(
        functools.partial(_gmlp_kernel, tm=tm),
        grid=(n_tok // tm,),
        in_specs=[
            pl.BlockSpec((tm, 2 * A_WIDTH), lambda i: (i, MIX_UV // (2 * A_WIDTH))),
            pl.BlockSpec((tm, d), lambda i: (i, MIX_GATES // d)),
            pl.BlockSpec((1, A_WIDTH), lambda i: (0, 0)),
            pl.BlockSpec((1, A_WIDTH), lambda i: (0, 0)),
            pl.BlockSpec((A_GROUPS, CHUNK, CHUNK), lambda i: (0, 0, 0)),
            pl.BlockSpec((CHUNK, A_GROUPS), lambda i: (0, 0)),
            pl.BlockSpec((A_WIDTH, d), lambda i: (0, 0)),
        ],
        out_specs=pl.BlockSpec((tm, d), lambda i: (i, 0)),
        out_shape=jax.ShapeDtypeStruct((n_tok, d), F32),
        scratch_shapes=[pltpu.VMEM((tm, A_WIDTH), BF16)],
        compiler_params=_params("parallel"),
        name="gmlp_branch",
    )(mix, mix, ln_g, ln_b, ws, bs_t, wp)


def _conf_kernel(cur_ref, prev_ref, next_ref, gate_ref, w_ref, cb_ref, lng_ref, lnb_ref, wp_ref,
                 o_ref, z_ref, zs_ref, *, tm, n_lat_blocks, per_seq_lat, per_seq_ctx):
    i = pl.program_id(0)
    is_ctx = i >= n_lat_blocks
    pos = jnp.where(is_ctx, (i - n_lat_blocks) % per_seq_ctx, i % per_seq_lat)
    last = jnp.where(is_ctx, per_seq_ctx - 1, per_seq_lat - 1)

    def glu(ref):
        return ref[:, :C_WIDTH].astype(F32) * jax.nn.sigmoid(ref[:, C_WIDTH:].astype(F32))

    z_ref[0:CONV_HALO, :] = jnp.where(pos > 0, glu(prev_ref), 0.0)
    z_ref[CONV_HALO:CONV_HALO + tm, :] = glu(cur_ref)
    z_ref[CONV_HALO + tm:, :] = jnp.where(pos < last, glu(next_ref), 0.0)
    acc = jnp.zeros((tm, C_WIDTH), F32) + cb_ref[...]
    first = CONV_HALO - CONV_W // 2
    span = tm + (CONV_W // SUBLANES) * SUBLANES
    for r in range(SUBLANES):
        zs_ref[r] = z_ref[r:r + span, :]
    for r in range(SUBLANES):
        for off in range(first, first + CONV_W):
            if off % SUBLANES == r:
                k = off - first
                acc = acc + zs_ref[r, off - r:off - r + tm, :] * w_ref[k:k + 1, :]
    y = _layer_norm(acc) * lng_ref[...] + lnb_ref[...]
    y = y * jax.nn.sigmoid(y)
    o_ref[...] = jax.nn.sigmoid(gate_ref[...]) * _dot(y.astype(BF16), wp_ref[...])


def _conf_branch(tok, mix, conv_w, conv_b, ln_g, ln_b, wp, n_tok, tm):
    d = D_MODEL
    glu_col = MIX_GLU // (2 * C_WIDTH)
    hb = tm // CONV_HALO
    n_halo_blocks = n_tok // CONV_HALO
    kernel = functools.partial(_conf_kernel, tm=tm, n_lat_blocks=tok.n_lat // tm,
                               per_seq_lat=tok.seq // tm, per_seq_ctx=tok.ctx_len // tm)
    return pl.pallas_call(
        kernel,
        grid=(n_tok // tm,),
        in_specs=[
            pl.BlockSpec((tm, 2 * C_WIDTH), lambda i: (i, glu_col)),
            pl.BlockSpec((CONV_HALO, 2 * C_WIDTH), lambda i: (jnp.maximum(i * hb - 1, 0), glu_col)),
            pl.BlockSpec((CONV_HALO, 2 * C_WIDTH),
                         lambda i: (jnp.minimum((i + 1) * hb, n_halo_blocks - 1), glu_col)),
            pl.BlockSpec((tm, d), lambda i: (i, MIX_GATES // d + 2)),
            pl.BlockSpec((CONV_W + 1, C_WIDTH), lambda i: (0, 0)),
            pl.BlockSpec((1, C_WIDTH), lambda i: (0, 0)),
            pl.BlockSpec((1, C_WIDTH), lambda i: (0, 0)),
            pl.BlockSpec((1, C_WIDTH), lambda i: (0, 0)),
            pl.BlockSpec((C_WIDTH, d), lambda i: (0, 0)),
        ],
        out_specs=pl.BlockSpec((tm, d), lambda i: (i, 0)),
        out_shape=jax.ShapeDtypeStruct((n_tok, d), F32),
        scratch_shapes=[pltpu.VMEM((tm + 2 * CONV_HALO, C_WIDTH), F32),
                        pltpu.VMEM((SUBLANES, tm + (CONV_W // SUBLANES) * SUBLANES, C_WIDTH), F32)],
        compiler_params=_params("parallel"),
        name="conformer_branch",
    )(mix, mix, mix, mix, conv_w, conv_b, ln_g, ln_b, wp)


def _qkv_prep_kernel(qkv_ref, cos_ref, sin_ref, bd_ref, gq_ref, gk_ref, q_ref, k_ref, v_ref, *, tm):
    lane = lax.broadcasted_iota(jnp.int32, (tm, LANES), 1)
    first_half = (lane % HEAD_DIM) < (HEAD_DIM // 2)
    low_head = lane < HEAD_DIM
    cos = cos_ref[...]
    sin = sin_ref[...]
    bd = bd_ref[...]

    def norm_rope(x, g):
        ms = _dot((x * x).astype(BF16), bd)
        xn = x * lax.rsqrt(ms + EPS) * g
        swapped = jnp.where(first_half, pltpu.roll(xn, LANES - HEAD_DIM // 2, 1),
                            pltpu.roll(xn, HEAD_DIM // 2, 1))
        return xn * cos + swapped * sin

    def head_masked(y, base, ref, fill_low, fill_high):
        rolled = pltpu.roll(y, HEAD_DIM, 1)
        ref[:, base:base + LANES] = jnp.where(low_head, y, fill_high).astype(BF16)
        ref[:, base + LANES:base + 2 * LANES] = jnp.where(low_head, fill_low, rolled).astype(BF16)
        ref[:, base + 2 * LANES:base + 3 * LANES] = jnp.where(low_head, rolled, fill_high).astype(BF16)
        ref[:, base + 3 * LANES:base + 4 * LANES] = jnp.where(low_head, fill_low, y).astype(BF16)

    zero = jnp.zeros((tm, LANES), F32)
    ones_low = jnp.where(lane == ONES_LANE_LOW, 1.0, 0.0)
    ones_high = jnp.where(lane == ONES_LANE_HIGH, 1.0, 0.0)
    scale = HEAD_DIM ** -0.5 * LOG2_E
    for c in range(Q_DIM // LANES):
        y = norm_rope(qkv_ref[:, c * LANES:(c + 1) * LANES], gq_ref[...])
        q_ref[:, c * LANES:(c + 1) * LANES] = (y * scale).astype(BF16)
    for c in range(KV_DIM // LANES):
        y = norm_rope(qkv_ref[:, Q_DIM + c * LANES:Q_DIM + (c + 1) * LANES], gk_ref[...])
        head_masked(y, c * 4 * LANES, k_ref, zero, zero)
        head_masked(qkv_ref[:, Q_DIM + KV_DIM + c * LANES:Q_DIM + KV_DIM + (c + 1) * LANES],
                    c * 4 * LANES, v_ref, ones_low, ones_high)


def _qkv_prep(tok, qkv, cos_t, sin_t, bd, gq, gk, tm):
    n_tok = qkv.shape[0]
    n_lat_blocks = tok.n_lat // tm
    per_seq = tok.seq // tm
    ext = N_KV_HEADS * 2 * LANES

    def tab(i):
        return (jnp.where(i < n_lat_blocks, i % per_seq, per_seq), 0)

    return pl.pallas_call(
        functools.partial(_qkv_prep_kernel, tm=tm),
        grid=(n_tok // tm,),
        in_specs=[
            pl.BlockSpec((tm, Q_DIM + 2 * KV_DIM), lambda i: (i, 0)),
            pl.BlockSpec((tm, LANES), tab),
            pl.BlockSpec((tm, LANES), tab),
            pl.BlockSpec((LANES, LANES), lambda i: (0, 0)),
            pl.BlockSpec((1, LANES), lambda i: (0, 0)),
            pl.BlockSpec((1, LANES), lambda i: (0, 0)),
        ],
        out_specs=[
            pl.BlockSpec((tm, Q_DIM), lambda i: (i, 0)),
            pl.BlockSpec((tm, ext), lambda i: (i, 0)),
            pl.BlockSpec((tm, ext), lambda i: (i, 0)),
        ],
        out_shape=[
            jax.ShapeDtypeStruct((n_tok, Q_DIM), BF16),
            jax.ShapeDtypeStruct((n_tok, ext), BF16),
            jax.ShapeDtypeStruct((n_tok, ext), BF16),
        ],
        compiler_params=_params("parallel"),
        name="qkv_prep",
    )(qkv, cos_t, sin_t, bd, gq, gk)


def _attn_kernel(q_ref, kc_ref, vc_ref, *rest, with_latent):
    if with_latent:
        kl_ref, vl_ref, o_ref = rest
    else:
        (o_ref,) = rest
    low_head = lax.broadcasted_iota(jnp.int32, (q_ref.shape[0], LANES), 1) < HEAD_DIM
    for pair in range(N_HEADS // 2):
        qc = q_ref[:, pair * LANES:(pair + 1) * LANES]
        outs = []
        for par in range(2):
            head = 2 * pair + par
            col = (head // (N_HEADS // N_KV_HEADS)) * 2 * LANES + par * LANES
            s_c = _dot_nt(qc, kc_ref[:, col:col + LANES])
            m = jnp.max(s_c, -1, keepdims=True)
            if with_latent:
                s_l = _dot_nt(qc, kl_ref[:, col:col + LANES])
                m = jnp.maximum(m, jnp.max(s_l, -1, keepdims=True))
            o = _dot(jnp.exp2((s_c - m).astype(BF16)), vc_ref[:, col:col + LANES])
            if with_latent:
                o = o + _dot(jnp.exp2((s_l - m).astype(BF16)), vl_ref[:, col:col + LANES])
            ones_lane = ONES_LANE_HIGH if par == 0 else ONES_LANE_LOW
            outs.append(o / o[:, ones_lane:ones_lane + 1])
        o_ref[:, pair * LANES:(pair + 1) * LANES] = jnp.where(low_head, outs[0], outs[1]).astype(o_ref.dtype)


def _attention(tok, q, k_ext, v_ext, latent, tq):
    batch, seq, ctx_len = tok.batch, tok.seq, tok.ctx_len
    ext = k_ext.shape[1]
    ctx_block0 = tok.n_lat // ctx_len
    if latent:
        per = seq // tq
        grid = (batch, per)
        q_map = lambda b, i: (b * per + i, 0)
        n_out = tok.n_lat
        out_map = q_map
    else:
        per = ctx_len // tq
        q0 = tok.n_lat // tq
        grid = (batch, per)
        q_map = lambda b, i: (q0 + b * per + i, 0)
        n_out = batch * ctx_len
        out_map = lambda b, i: (b * per + i, 0)
    in_specs = [
        pl.BlockSpec((tq, Q_DIM), q_map),
        pl.BlockSpec((ctx_len, ext), lambda b, i: (ctx_block0 + b, 0)),
        pl.BlockSpec((ctx_len, ext), lambda b, i: (ctx_block0 + b, 0)),
    ]
    args = [q, k_ext, v_ext]
    if latent:
        in_specs += [pl.BlockSpec((seq, ext), lambda b, i: (b, 0)),
                     pl.BlockSpec((seq, ext), lambda b, i: (b, 0))]
        args += [k_ext, v_ext]
    return pl.pallas_call(
        functools.partial(_attn_kernel, with_latent=latent),
        grid=grid,
        in_specs=in_specs,
        out_specs=pl.BlockSpec((tq, Q_DIM), out_map),
        out_shape=jax.ShapeDtypeStruct((n_out, Q_DIM), BF16),
        compiler_params=_params("parallel", "parallel"),
        name="attention_latent" if latent else "attention_context",
    )(*args)


def _merge_kernel(x_ref, a_ref, att_ref, c_ref, gate_ref, g1_ref, bwp_ref, wout_ref, lng_ref,
                  lnb_ref, o_ref, *, alpha):
    gate = jax.nn.sigmoid(gate_ref[...].astype(F32))
    m = a_ref[...] + c_ref[...] + gate * _dot(att_ref[...], bwp_ref[...])
    y = _dot(m.astype(BF16), wout_ref[...])
    z = alpha * x_ref[...] + g1_ref[0] * y
    o_ref[...] = _layer_norm(z) * lng_ref[...] + lnb_ref[...]


def _merge(tok, x_all, ha, att, hc, gates, g1, b_wp, w_out, ln_g, ln_b, n_tok, tm, alpha):
    d = D_MODEL
    row = tok.mod_row(tm)
    tok_spec = pl.BlockSpec((tm, d), lambda i: (i, 0))
    vec_spec = pl.BlockSpec((1, d), lambda i: (0, 0))
    mat_spec = pl.BlockSpec((d, d), lambda i: (0, 0))
    return pl.pallas_call(
        functools.partial(_merge_kernel, alpha=alpha),
        grid=(n_tok // tm,),
        in_specs=[tok_spec, tok_spec, tok_spec, tok_spec,
                  pl.BlockSpec((tm, d), lambda i: (i, MIX_GATES // d + 1)),
                  pl.BlockSpec((1, 1, d), lambda i: (row(i), 0, 0)),
                  mat_spec, mat_spec, vec_spec, vec_spec],
        out_specs=tok_spec,
        out_shape=jax.ShapeDtypeStruct((n_tok, d), F32),
        compiler_params=_params("parallel"),
        name="merge",
    )(x_all, ha, att, hc, gates, g1, b_wp, w_out, ln_g, ln_b)


def _peer_query_kernel(x_ref, sc_ref, sh_ref, wq_ref, keys_ref, h2t_ref, st_ref, *, tm):
    h2 = _layer_norm(x_ref[...]) * (1.0 + sc_ref[0]) + sh_ref[0]
    h2t_ref[...] = h2.T.astype(BF16)
    q = _dot(h2.astype(BF16), wq_ref[...])
    for hp in range(2 * PEER_HEADS):
        qhp = q[:, hp * PEER_HALF:(hp + 1) * PEER_HALF].astype(BF16)
        st = _dot_nt(keys_ref[hp % 2], qhp)
        for lt in range(tm // LANES):
            st_ref[hp, lt] = st[:, lt * LANES:(lt + 1) * LANES]


def _peer_query(tok, x_all, sc, sh, wq, keys, n_tok, tm):
    d = D_MODEL
    row = tok.mod_row(tm)
    return pl.pallas_call(
        functools.partial(_peer_query_kernel, tm=tm),
        grid=(n_tok // tm,),
        in_specs=[
            pl.BlockSpec((tm, d), lambda i: (i, 0)),
            pl.BlockSpec((1, 1, d), lambda i: (row(i), 0, 0)),
            pl.BlockSpec((1, 1, d), lambda i: (row(i), 0, 0)),
            pl.BlockSpec((d, PEER_HEADS * PEER_QDIM), lambda i: (0, 0)),
            pl.BlockSpec((2, N_KEYS, PEER_HALF), lambda i: (0, 0, 0)),
        ],
        out_specs=[
            pl.BlockSpec((d, tm), lambda i: (0, i)),
            pl.BlockSpec((2 * PEER_HEADS, tm // LANES, N_KEYS, LANES), lambda i: (0, i, 0, 0)),
        ],
        out_shape=[
            jax.ShapeDtypeStruct((d, n_tok), BF16),
            jax.ShapeDtypeStruct((2 * PEER_HEADS, n_tok // LANES, N_KEYS, LANES), F32),
        ],
        compiler_params=_params("parallel"),
        name="peer_query",
    )(x_all, sc, sh, wq, keys)


def _peer_topk_kernel(st_ref, n_ref, c_ref, rank_ref, r_ref, top_ref, *, n_tiles):
    neg = -jnp.inf

    def tile(idx, carry):
        h = idx // n_tiles
        lt = idx % n_tiles
        s1 = st_ref[2 * h, lt]
        s2 = st_ref[2 * h + 1, lt]
        cur = s1
        for k in range(PEER_TOPK):
            m = jnp.max(cur, axis=0, keepdims=True)
            top_ref[0, k:k + 1, :] = m
            cur = jnp.where(cur >= m, neg, cur)
        cur = s2
        rank = jnp.full(s2.shape, float(N_KEYS), F32)
        for k in range(PEER_TOPK):
            m = jnp.max(cur, axis=0, keepdims=True)
            top_ref[1, k:k + 1, :] = m
            hit = cur >= m
            rank = jnp.where(hit, float(k), rank)
            cur = jnp.where(hit, neg, cur)
        v1 = top_ref[0]
        v2 = top_ref[1]
        pieces = [v1[0:1] + v2]
        for a in range(1, 8):
            pieces.append(v1[a:a + 1] + v2[0:8])
        pieces.append(v1[8:16] + v2[0:1])
        cand = jnp.concatenate(pieces, axis=0)
        m0 = jnp.max(cand, axis=0, keepdims=True)
        cand = jnp.where(cand >= m0, neg, cand)
        zsum = jnp.ones_like(m0)
        m = m0
        for k in range(1, PEER_TOPK):
            m = jnp.max(cand, axis=0, keepdims=True)
            zsum = zsum + jnp.exp(m - m0)
            cand = jnp.where(cand >= m, neg, cand)
        thr = m
        cnt_top = jnp.zeros(v1.shape, F32)
        for k in range(PEER_TOPK):
            cnt_top = cnt_top + jnp.where(v1 + v2[k:k + 1] >= thr, 1.0, 0.0)
        cnt = jnp.zeros(s1.shape, F32)
        for a in range(PEER_TOPK):
            cnt = jnp.where(s1 == v1[a:a + 1], cnt_top[a:a + 1], cnt)
        lanes = pl.ds(pl.multiple_of(lt * LANES, LANES), LANES)
        n_ref[h, :, lanes] = cnt
        c_ref[h, :, lanes] = jnp.exp(s1 - v1[0:1])
        rank_ref[h, lt] = rank.astype(BF16)
        r_ref[h, lt] = (jnp.exp(s2 - v2[0:1]) * (0.5 / zsum)).astype(BF16)
        return carry

    lax.fori_loop(0, PEER_HEADS * n_tiles, tile, 0)


def _peer_topk(st, tl):
    n_tiles_all = st.shape[1]
    n_tok = n_tiles_all * LANES
    n_tiles = tl // LANES
    row_spec = pl.BlockSpec((PEER_HEADS, N_KEYS, tl), lambda i: (0, 0, i))
    row_shape = jax.ShapeDtypeStruct((PEER_HEADS, N_KEYS, n_tok), F32)
    tile_spec = pl.BlockSpec((PEER_HEADS, n_tiles, N_KEYS, LANES), lambda i: (0, i, 0, 0))
    tile_shape = jax.ShapeDtypeStruct((PEER_HEADS, n_tiles_all, N_KEYS, LANES), BF16)
    return pl.pallas_call(
        functools.partial(_peer_topk_kernel, n_tiles=n_tiles),
        grid=(n_tok // tl,),
        in_specs=[pl.BlockSpec((2 * PEER_HEADS, n_tiles, N_KEYS, LANES), lambda i: (0, i, 0, 0))],
        out_specs=[row_spec, row_spec, tile_spec, tile_spec],
        out_shape=[row_shape, row_shape, tile_shape, tile_shape],
        scratch_shapes=[pltpu.VMEM((2, PEER_TOPK, LANES), F32)],
        compiler_params=_params("parallel"),
        name="peer_topk",
    )(st)


def _peer_main_kernel(h2t_ref, u_ref, vt_ref, rank_ref, r_ref, n_ref, c_ref, x_ref, g2_ref, lng_ref,
                      lnb_ref, o_ref, acc_ref, w_ref, st_ref, *, n_slab, tb, alpha):
    k = pl.program_id(1)

    @pl.when(k == 0)
    def _():
        acc_ref[...] = jnp.zeros_like(acc_ref)

    st = _dot(u_ref[...], h2t_ref[...])
    for lt in range(tb // LANES):
        st_ref[lt] = st[:, lt * LANES:(lt + 1) * LANES]

    def row_bf16(ref, h, sl, lanes):
        return jnp.broadcast_to(ref[h, sl:sl + 1, lanes], (BF16_ROWS, LANES)).astype(BF16)

    groups = [slice(r0, r0 + BF16_ROWS) for r0 in range(0, N_KEYS, BF16_ROWS)]
    for sl in range(n_slab):
        for lt in range(tb // LANES):
            lanes = slice(lt * LANES, (lt + 1) * LANES)
            g = [None] * len(groups)
            for h in range(PEER_HEADS):
                cnt = row_bf16(n_ref, h, sl, lanes)
                cfac = row_bf16(c_ref, h, sl, lanes)
                for gi, rows in enumerate(groups):
                    picked_c = jnp.minimum(jnp.maximum(cnt - rank_ref[h, lt, rows, :], 0.0), cfac)
                    term = picked_c * r_ref[h, lt, rows, :]
                    g[gi] = term if g[gi] is None else g[gi] + term
            for gi, rows in enumerate(groups):
                e0 = sl * N_KEYS + rows.start
                x = st_ref[lt, e0:e0 + BF16_ROWS, :].astype(BF16)
                t = jnp.tanh(x * (GELU_K1 + GELU_K2 * (x * x)))
                w_ref[e0:e0 + BF16_ROWS, lanes] = (x * (1.0 + t)) * g[gi]
    acc_ref[...] += _dot(vt_ref[...], w_ref[...])

    @pl.when(k == pl.num_programs(1) - 1)
    def _():
        z = alpha * x_ref[...] + g2_ref[0] * acc_ref[...].T
        o_ref[...] = _layer_norm(z) * lng_ref[...] + lnb_ref[...]


def _peer_main(tok, x_all, h2t, n_cnt, c_fac, rank, r_fac, g2, u_tab, vt_tab, ln_g, ln_b, n_tok, tb,
               n_slab, alpha):
    d = D_MODEL
    ec = n_slab * N_KEYS
    row = tok.mod_row(tb)
    tile_spec = pl.BlockSpec((PEER_HEADS, tb // LANES, N_KEYS, LANES), lambda t, k: (0, t, 0, 0))
    return pl.pallas_call(
        functools.partial(_peer_main_kernel, n_slab=n_slab, tb=tb, alpha=alpha),
        grid=(n_tok // tb, N_EXPERTS // ec),
        in_specs=[
            pl.BlockSpec((d, tb), lambda t, k: (0, t)),
            pl.BlockSpec((ec, d), lambda t, k: (k, 0)),
            pl.BlockSpec((d, ec), lambda t, k: (0, k)),
            tile_spec,
            tile_spec,
            pl.BlockSpec((PEER_HEADS, n_slab, tb), lambda t, k: (0, k, t)),
            pl.BlockSpec((PEER_HEADS, n_slab, tb), lambda t, k: (0, k, t)),
            pl.BlockSpec((tb, d), lambda t, k: (t, 0)),
            pl.BlockSpec((1, 1, d), lambda t, k: (row(t), 0, 0)),
            pl.BlockSpec((1, d), lambda t, k: (0, 0)),
            pl.BlockSpec((1, d), lambda t, k: (0, 0)),
        ],
        out_specs=pl.BlockSpec((tb, d), lambda t, k: (t, 0)),
        out_shape=jax.ShapeDtypeStruct((n_tok, d), F32),
        scratch_shapes=[pltpu.VMEM((d, tb), F32), pltpu.VMEM((ec, tb), BF16),
                        pltpu.VMEM((tb // LANES, ec, LANES), F32)],
        compiler_params=_params("parallel", "arbitrary"),
        name="peer_main",
    )(h2t, u_tab, vt_tab, rank, r_fac, n_cnt, c_fac, x_all, g2, ln_g, ln_b)


def _rope_tables(seq, tm):
    rows = seq // GRID_W
    row = jnp.repeat(jnp.arange(rows, dtype=F32), GRID_W)
    col = jnp.tile(jnp.arange(GRID_W, dtype=F32), rows)
    n_freq = HEAD_DIM // 4
    inv = ROPE_THETA ** (-jnp.arange(n_freq, dtype=F32) / n_freq)
    ang = jnp.concatenate([row[:, None] * inv, col[:, None] * inv], -1)
    cos, sin = jnp.cos(ang), jnp.sin(ang)
    cos_t = jnp.tile(jnp.concatenate([cos, cos], -1), (1, LANES // HEAD_DIM))
    sin_t = jnp.tile(jnp.concatenate([-sin, sin], -1), (1, LANES // HEAD_DIM))
    cos_t = jnp.concatenate([cos_t, jnp.ones((tm, LANES), F32)], 0)
    sin_t = jnp.concatenate([sin_t, jnp.zeros((tm, LANES), F32)], 0)
    return cos_t, sin_t


def _row(v):
    return v.reshape(1, -1)


def kernel(x, c, ctx, c_ctx, w_mod, b_mod, w_in, a_ln_g, a_ln_b, a_ws, a_bs, a_wp, q_norm_g, k_norm_g, b_wp, c_conv_w, c_conv_b, c_ln_g, c_ln_b, c_wp, w_out, ln1_g, ln1_b, peer_wq, peer_keys, peer_u, peer_v, ln2_g, ln2_b):
    batch, seq, d = x.shape
    ctx_len = ctx.shape[1]
    depth = w_mod.shape[0]
    assert d == D_MODEL and batch < MOD_ROWS
    tok = _Tokens(batch, seq, ctx_len)
    alpha = float((2 * depth) ** 0.25)

    tm = min(256, ctx_len)
    tq = min(256, ctx_len)
    tb = min(512, batch * ctx_len)
    n_slab = 16
    assert seq % tm == 0 and ctx_len % tm == 0 and seq % tb == 0 and (batch * ctx_len) % tb == 0

    x_all = jnp.concatenate([x.reshape(batch * seq, d), ctx.reshape(batch * ctx_len, d)], 0)
    c_rows = jnp.concatenate([c, c_ctx[None], jnp.zeros((MOD_ROWS - batch - 1, d), F32)], 0)
    mod = _modulation(c_rows, w_mod, b_mod)

    cos_t, sin_t = _rope_tables(seq, tm)
    head_perm = np.concatenate([np.arange(0, HEAD_DIM, 2), np.arange(1, HEAD_DIM, 2)])
    q_cols = 2 * A_WIDTH + (np.arange(N_HEADS)[:, None] * HEAD_DIM + head_perm[None]).reshape(-1)
    k_cols = 2 * A_WIDTH + Q_DIM + (np.arange(N_KV_HEADS)[:, None] * HEAD_DIM + head_perm[None]).reshape(-1)
    v_cols = 2 * A_WIDTH + Q_DIM + KV_DIM + np.arange(KV_DIM)
    qkv_cols = np.concatenate([q_cols, k_cols, v_cols])
    o_glu = 2 * A_WIDTH + Q_DIM + 2 * KV_DIM
    lane_head = np.arange(LANES) // HEAD_DIM
    bd = jnp.asarray((lane_head[:, None] == lane_head[None]) / HEAD_DIM, BF16)

    for i in range(depth):
        last = i == depth - 1
        n_mix = tok.n_lat if last else tok.n_all

        def mod_chunk(j, i=i):
            return mod[i, :, j * d:(j + 1) * d].reshape(MOD_ROWS, 1, d)

        sh1, sc1, g1, sh2, sc2, g2 = (mod_chunk(j) for j in range(6))
        w_i = w_in[i]
        w_qkv = w_i[:, qkv_cols].astype(BF16)
        w_mix = jnp.concatenate([w_i[:, :2 * A_WIDTH], w_i[:, o_glu:]], 1).astype(BF16)

        mix = _ln_mod_matmul(tok, x_all, n_mix, sc1, sh1, w_mix, tb, w_mix.shape[1] // 2, BF16)
        qkv = _ln_mod_matmul(tok, x_all, tok.n_all, sc1, sh1, w_qkv, tb, Q_DIM + 2 * KV_DIM)

        ha = _gmlp_branch(mix, _row(a_ln_g[i]), _row(a_ln_b[i]), a_ws[i].astype(BF16),
                          a_bs[i].T, a_wp[i].astype(BF16), n_mix, tm)
        conv_w = jnp.concatenate([c_conv_w[i], jnp.zeros((1, C_WIDTH), F32)], 0)
        hc = _conf_branch(tok, mix, conv_w, _row(c_conv_b[i]), _row(c_ln_g[i]),
                          _row(c_ln_b[i]), c_wp[i].astype(BF16), n_mix, tm)
        gq = jnp.tile(q_norm_g[i][head_perm], LANES // HEAD_DIM).reshape(1, LANES)
        gk = jnp.tile(k_norm_g[i][head_perm], LANES // HEAD_DIM).reshape(1, LANES)
        q_r, k_ext, v_ext = _qkv_prep(tok, qkv, cos_t, sin_t, bd, gq, gk, tm)
        att = _attention(tok, q_r, k_ext, v_ext, True, tq)
        if not last:
            att = jnp.concatenate([att, _attention(tok, q_r, k_ext, v_ext, False, tq)], 0)
        x_mix = _merge(tok, x_all, ha, att, hc, mix, g1, b_wp[i].astype(BF16),
                       w_out[i].astype(BF16), _row(ln1_g[i]), _row(ln1_b[i]), n_mix, tm, alpha)

        h2t, st = _peer_query(tok, x_mix, sc2, sh2, peer_wq[i].astype(BF16),
                              peer_keys[i].astype(BF16), n_mix, tm)
        n_cnt, c_fac, rank, r_fac = _peer_topk(st, tb)
        x_new = _peer_main(tok, x_mix, h2t, n_cnt, c_fac, rank, r_fac, g2, peer_u[i].astype(BF16),
                           peer_v[i].T.astype(BF16), _row(ln2_g[i]), _row(ln2_b[i]), n_mix, tb,
                           n_slab, alpha)
        if last:
            return x_new.reshape(batch, seq, d)
        x_all = x_new
```
